```python
import jax, jax.numpy as jnp
from jax import lax
import numpy as np

D_MODEL = 1024
BATCH = 4
SEQ = 4096
DEPTH = 2

GRID_W = 64
HEAD_DIM = 64
NA_HEADS = (D_MODEL // 2) // HEAD_DIM
GQA_Q_HEADS = (D_MODEL // 2) // HEAD_DIM
GQA_KV_HEADS = GQA_Q_HEADS // 4
NA_KH_MAX = 8
NA_KW = 16
ROPE_THETA = 10000.0
Q_BLOCK = 128
NA_WIDTH = NA_HEADS * HEAD_DIM
GQA_WIDTH = GQA_Q_HEADS * HEAD_DIM
KV_WIDTH = GQA_KV_HEADS * HEAD_DIM
MIX_WIDTH = NA_WIDTH + GQA_WIDTH
IN_SPLITS = [NA_WIDTH, NA_WIDTH, NA_WIDTH, GQA_WIDTH, KV_WIDTH, KV_WIDTH]
IN_WIDTH = sum(IN_SPLITS)
IN_SPLIT_POINTS = [int(p) for p in np.cumsum(IN_SPLITS)[:-1]]
MEM_LEN = 256
MEM_HEADS = 4
MEM_HEAD_DIM = 128
MEM_WIDTH = MEM_HEADS * MEM_HEAD_DIM
D_FF = ((8 * D_MODEL // 3 + 127) // 128) * 128
CONV_W = 3
EPS = 1e-6
NEG_INF = -1e30

kernel_name = "hybrid_na_gqa_memxattn_convffn_encoder"


def rms_norm(x, g):
    x32 = x.astype(jnp.float32)
    y = x32 * lax.rsqrt(jnp.mean(x32 * x32, axis=-1, keepdims=True) + EPS)
    return (y * g.astype(jnp.float32)).astype(x.dtype)


def rope_tables(s):
    t = jnp.arange(s)
    pos = jnp.stack([t // GRID_W, t % GRID_W], axis=-1).astype(jnp.float32)
    n_f = HEAD_DIM // 4
    inv_freq = ROPE_THETA ** (-jnp.arange(n_f, dtype=jnp.float32) / n_f)
    ang = pos[:, :, None] * inv_freq
    return jnp.cos(ang), jnp.sin(ang)


def apply_axial_rope(x, cos, sin):
    b, s, h, dh = x.shape
    xr = x.astype(jnp.float32).reshape(b, s, h, 2, 2, dh // 4)
    x1, x2 = xr[..., 0, :], xr[..., 1, :]
    c = cos[None, :, None]
    sn = sin[None, :, None]
    out = jnp.stack([x1 * c - x2 * sn, x1 * sn + x2 * c], axis=-2)
    return out.reshape(b, s, h, dh).astype(x.dtype)


def neighbourhood_attention(q, k, v, rpb, rows):
    b, s, h, dh = q.shape
    kh = min(NA_KH_MAX, rows)
    q = q.reshape(b, rows, GRID_W, h, dh)
    k = k.reshape(b, rows, GRID_W, h, dh)
    v = v.reshape(b, rows, GRID_W, h, dh)
    r = jnp.arange(rows)
    row_start = jnp.clip(r - kh // 2, 0, rows - kh)
    row_idx = row_start[:, None] + jnp.arange(kh)[None, :]
    k_rows = k[:, row_idx]
    v_rows = v[:, row_idx]
    scores = jnp.einsum('brqhd,brikhd->bhrqik', q, k_rows,
                        preferred_element_type=jnp.float32) * (dh ** -0.5)
    c = jnp.arange(GRID_W)
    col_start = jnp.clip(c - NA_KW // 2, 0, GRID_W - NA_KW)
    col_valid = (c[None, :] >= col_start[:, None]) & (c[None, :] < col_start[:, None] + NA_KW)
    row_off = row_idx - r[:, None] + (NA_KH_MAX - 1)
    col_off = jnp.clip(c[None, :] - c[:, None], -(NA_KW - 1), NA_KW - 1) + (NA_KW - 1)
    bias = rpb[:, row_off[:, None, :, None], col_off[None, :, None, :]]
    scores = scores + bias[None].astype(jnp.float32)
    scores = jnp.where(col_valid[None, None, None, :, None, :], scores, NEG_INF)
    p = jax.nn.softmax(scores, axis=(-2, -1))
    o = jnp.einsum('bhrqik,brikhd->brqhd', p.astype(v.dtype), v_rows)
    return o.reshape(b, s, h * dh)


def gqa_block_attention(q, k, v):
    b, s, hq, dh = q.shape
    hkv = k.shape[2]
    g = hq // hkv
    nblk = s // Q_BLOCK
    qb = q.reshape(b, nblk, Q_BLOCK, hkv, g, dh).transpose(1, 0, 2, 3, 4, 5)
    scale = dh ** -0.5

    def one_block(qi):
        sc = jnp.einsum('bqkgd,bskd->bkgqs', qi, k, preferred_element_type=jnp.float32) * scale
        p = jax.nn.softmax(sc, axis=-1)
        return jnp.einsum('bkgqs,bskd->bqkgd', p.astype(v.dtype), v)

    o = lax.map(one_block, qb)
    return o.transpose(1, 0, 2, 3, 4, 5).reshape(b, s, hq * dh)


def hybrid_mixer(h, w_in, rpb, q_norm, k_norm, w_out, rows, cos, sin):
    b, s, _ = h.shape
    proj = h @ w_in
    na_q, na_k, na_v, g_q, g_k, g_v = jnp.split(proj, IN_SPLIT_POINTS, axis=-1)
    na_shape = (b, s, NA_HEADS, HEAD_DIM)
    na_out = neighbourhood_attention(na_q.reshape(na_shape), na_k.reshape(na_shape),
                                     na_v.reshape(na_shape), rpb, rows)
    g_q = apply_axial_rope(rms_norm(g_q.reshape(b, s, GQA_Q_HEADS, HEAD_DIM), q_norm), cos, sin)
    g_k = apply_axial_rope(rms_norm(g_k.reshape(b, s, GQA_KV_HEADS, HEAD_DIM), k_norm), cos, sin)
    g_v = g_v.reshape(b, s, GQA_KV_HEADS, HEAD_DIM)
    gqa_out = gqa_block_attention(g_q, g_k, g_v)
    return jnp.concatenate([na_out, gqa_out], axis=-1) @ w_out


def memory_cross_attention(h, mem_n, w_q, w_kv, w_o):
    b, s, _ = h.shape
    m = mem_n.shape[1]
    q = (h @ w_q).reshape(b, s, MEM_HEADS, MEM_HEAD_DIM)
    kv = (mem_n @ w_kv).reshape(b, m, 2, MEM_HEADS, MEM_HEAD_DIM)
    k, v = kv[:, :, 0], kv[:, :, 1]
    sc = jnp.einsum('bqhd,bmhd->bhqm', q, k, preferred_element_type=jnp.float32) * (MEM_HEAD_DIM ** -0.5)
    p = jax.nn.softmax(sc, axis=-1)
    o = jnp.einsum('bhqm,bmhd->bqhd', p.astype(v.dtype), v).reshape(b, s, MEM_WIDTH)
    return o @ w_o


def conv_ffn(h, w_up, conv_w, conv_b, w_down):
    u = h @ w_up
    up = jnp.pad(u, ((0, 0), (1, 1), (0, 0)))
    u = up[:, :-2] * conv_w[0] + up[:, 1:-1] * conv_w[1] + up[:, 2:] * conv_w[2] + conv_b
    gate, val = jnp.split(u, 2, axis=-1)
    return (jax.nn.silu(gate) * val) @ w_down


def setup_inputs(seed: int = 0) -> dict:
    key = jax.random.key(seed)
    ks = jax.random.split(key, 24)
    f32 = jnp.float32

    def normal(k, shape, scale):
        return jax.random.normal(k, shape, f32) * scale

    def gain(k, shape):
        return 1.0 + 0.02 * jax.random.normal(k, shape, f32)

    L = DEPTH
    return {
        "x": normal(ks[0], (BATCH, SEQ, D_MODEL), 1.0),
        "mem": normal(ks[1], (BATCH, MEM_LEN, D_MODEL), 1.0),
        "norm_mix": gain(ks[2], (L, D_MODEL)),
        "w_in": normal(ks[3], (L, D_MODEL, IN_WIDTH), D_MODEL ** -0.5),
        "na_rpb": normal(ks[4], (L, NA_HEADS, 2 * NA_KH_MAX - 1, 2 * NA_KW - 1), 0.1),
        "gqa_q_norm": gain(ks[5], (L, HEAD_DIM)),
        "gqa_k_norm": gain(ks[6], (L, HEAD_DIM)),
        "w_out": normal(ks[7], (L, MIX_WIDTH, D_MODEL), MIX_WIDTH ** -0.5),
        "norm_mem_q": gain(ks[8], (L, D_MODEL)),
        "norm_mem_kv": gain(ks[9], (L, D_MODEL)),
        "w_mem_q": normal(ks[10], (L, D_MODEL, MEM_WIDTH), D_MODEL ** -0.5),
        "w_mem_kv": normal(ks[11], (L, D_MODEL, 2 * MEM_WIDTH), D_MODEL ** -0.5),
        "w_mem_o": normal(ks[12], (L, MEM_WIDTH, D_MODEL), MEM_WIDTH ** -0.5),
        "norm_ffn": gain(ks[13], (L, D_MODEL)),
        "w_up": normal(ks[14], (L, D_MODEL, 2 * D_FF), D_MODEL ** -0.5),
        "conv_w": normal(ks[15], (L, CONV_W, 2 * D_FF), CONV_W ** -0.5),
        "conv_b": normal(ks[16], (L, 2 * D_FF), 0.02),
        "w_down": normal(ks[17], (L, D_FF, D_MODEL), D_FF ** -0.5),
        "norm_final": gain(ks[18], (D_MODEL,)),
    }


def reference(x, mem, norm_mix, w_in, na_rpb, gqa_q_norm, gqa_k_norm, w_out,
              norm_mem_q, norm_mem_kv, w_mem_q, w_mem_kv, w_mem_o,
              norm_ffn, w_up, conv_w, conv_b, w_down, norm_final):
    s = x.shape[1]
    rows = s // GRID_W
    cos, sin = rope_tables(s)
    for l in range(DEPTH):
        x = x + hybrid_mixer(rms_norm(x, norm_mix[l]), w_in[l], na_rpb[l],
                             gqa_q_norm[l], gqa_k_norm[l], w_out[l], rows, cos, sin)
        x = x + memory_cross_attention(rms_norm(x, norm_mem_q[l]), rms_norm(mem, norm_mem_kv[l]),
                                       w_mem_q[l], w_mem_kv[l], w_mem_o[l])
        x = x + conv_ffn(rms_norm(x, norm_ffn[l]), w_up[l], conv_w[l], conv_b[l], w_down[l])
    return rms_norm(x, norm_final)
```

```python
import functools

import jax
import jax.numpy as jnp
from jax import lax
from jax.experimental import pallas as pl
from jax.experimental.pallas import tpu as pltpu

D_MODEL = 1024
GRID_W = 64
HEAD_DIM = 64
NA_HEADS = 8
GQA_Q_HEADS = 8
GQA_KV_HEADS = 2
NA_KH = 8
NA_KW = 16
ROPE_THETA = 10000.0
NA_WIDTH = NA_HEADS * HEAD_DIM
GQA_WIDTH = GQA_Q_HEADS * HEAD_DIM
KV_WIDTH = GQA_KV_HEADS * HEAD_DIM
MEM_HEADS = 4
MEM_HEAD_DIM = 128
MEM_WIDTH = MEM_HEADS * MEM_HEAD_DIM
D_FF = 2816
EPS = 1e-6
NEG_INF = -1e30

LANES = 128
VMEM_LIMIT = 56 * 1024 * 1024

TM = 512
HALO = 16
FF_CHUNK = 1408
TQ = 128
NA_ROWS_PER_STEP = 8

BF16 = jnp.bfloat16
F32 = jnp.float32


def _rms(x, g):
    return x * lax.rsqrt(jnp.mean(x * x, axis=-1, keepdims=True) + EPS) * g


def _dot(a, b):
    return jnp.dot(a, b, preferred_element_type=F32)


def _dot_nt(a, b):
    return lax.dot_general(a, b, (((1,), (1,)), ((), ())), preferred_element_type=F32)


def _const_spec(shape):
    nd = len(shape)
    return pl.BlockSpec(shape, lambda *_: (0,) * nd, pipeline_mode=pl.Buffered(1))


def _params(*sem):
    return pltpu.CompilerParams(dimension_semantics=sem, vmem_limit_bytes=VMEM_LIMIT)


def _in_proj_kernel(x_ref, g_ref, w_ref, naq_ref, nak_ref, nav_ref, gq_ref, gk_ref, gv_ref):
    h = _rms(x_ref[...], g_ref[...]).astype(BF16)
    o = 0
    naq_ref[...] = (_dot(h, w_ref[:, o:o + NA_WIDTH]) * (HEAD_DIM ** -0.5)).astype(BF16)
    o += NA_WIDTH
    nak_ref[...] = _dot(h, w_ref[:, o:o + NA_WIDTH]).astype(BF16)
    o += NA_WIDTH
    nav_ref[...] = _dot(h, w_ref[:, o:o + NA_WIDTH]).astype(BF16)
    o += NA_WIDTH
    gq_ref[...] = _dot(h, w_ref[:, o:o + GQA_WIDTH])
    o += GQA_WIDTH
    gk_ref[...] = _dot(h, w_ref[:, o:o + KV_WIDTH])
    o += KV_WIDTH
    gv_ref[...] = _dot(h, w_ref[:, o:o + KV_WIDTH]).astype(BF16)


def _in_proj(x2, g, w):
    n = x2.shape[0]
    in_width = w.shape[1]
    row = lambda width: pl.BlockSpec((TM, width), lambda i: (i, 0))
    return pl.pallas_call(
        _in_proj_kernel,
        grid=(n // TM,),
        in_specs=[row(D_MODEL), _const_spec((1, D_MODEL)), _const_spec((D_MODEL, in_width))],
        out_specs=[row(NA_WIDTH), row(NA_WIDTH), row(NA_WIDTH), row(GQA_WIDTH), row(KV_WIDTH), row(KV_WIDTH)],
        out_shape=[
            jax.ShapeDtypeStruct((n, NA_WIDTH), BF16),
            jax.ShapeDtypeStruct((n, NA_WIDTH), BF16),
            jax.ShapeDtypeStruct((n, NA_WIDTH), BF16),
            jax.ShapeDtypeStruct((n, GQA_WIDTH), F32),
            jax.ShapeDtypeStruct((n, KV_WIDTH), F32),
            jax.ShapeDtypeStruct((n, KV_WIDTH), BF16),
        ],
        compiler_params=_params("parallel"),
        name="in_proj",
    )(x2, g, w)


def _na_bias_table(rpb, rows):
    c = jnp.arange(GRID_W)
    col_start = jnp.clip(c - NA_KW // 2, 0, GRID_W - NA_KW)
    col_valid = (c[None, :] >= col_start[:, None]) & (c[None, :] < col_start[:, None] + NA_KW)
    col_off = jnp.clip(c[None, :] - c[:, None], -(NA_KW - 1), NA_KW - 1) + (NA_KW - 1)
    cls = jnp.arange(NA_KH)
    row_off = jnp.arange(NA_KH)[None, :] - cls[:, None] + (NA_KH - 1)
    bias = rpb[:, row_off[:, None, :, None], col_off[None, :, None, :]]
    bias = jnp.where(col_valid[None, None, :, None, :], bias, NEG_INF)
    bias = bias.transpose(1, 0, 2, 3, 4).reshape(NA_KH, NA_HEADS // 2, 2 * GRID_W, NA_KH * GRID_W)
    return bias.astype(F32)


def _na_kernel(q_ref, k_ref, v_ref, bias_ref, o_ref, *, rows):
    rb = pl.program_id(1)
    lo = lax.broadcasted_iota(jnp.int32, (GRID_W, LANES), 1) < HEAD_DIM
    win = NA_KH * GRID_W

    def body(i, carry):
        r = rb * NA_ROWS_PER_STEP + i
        row_start = jnp.clip(r - NA_KH // 2, 0, rows - NA_KH)
        cls = r - row_start
        kstart = pl.multiple_of(row_start * GRID_W, GRID_W)
        qstart = pl.multiple_of(i * GRID_W, GRID_W)
        for j in range(NA_HEADS // 2):
            cols = slice(j * LANES, (j + 1) * LANES)
            qp = q_ref[pl.ds(qstart, GRID_W), cols]
            zero = jnp.zeros_like(qp)
            qs = jnp.concatenate([jnp.where(lo, qp, zero), jnp.where(lo, zero, qp)], axis=0)
            kp = k_ref[pl.ds(kstart, win), cols]
            s = _dot_nt(qs, kp) + bias_ref[cls, j]
            m = jnp.max(s, axis=-1, keepdims=True)
            p = jnp.exp(s - m)
            l = jnp.sum(p, axis=-1, keepdims=True)
            vp = v_ref[pl.ds(kstart, win), cols]
            o = _dot(p.astype(BF16), vp) / l
            o_ref[pl.ds(qstart, GRID_W), cols] = jnp.where(lo, o[:GRID_W], o[GRID_W:]).astype(BF16)
        return carry

    lax.fori_loop(0, NA_ROWS_PER_STEP, body, 0)


def _na_attention(q, k, v, bias, batch, seq):
    rows = seq // GRID_W
    tokens = NA_ROWS_PER_STEP * GRID_W
    steps = rows // NA_ROWS_PER_STEP
    qspec = pl.BlockSpec((tokens, NA_WIDTH), lambda b, r: (b * steps + r, 0))
    kvspec = pl.BlockSpec((seq, NA_WIDTH), lambda b, r: (b, 0))
    return pl.pallas_call(
        functools.partial(_na_kernel, rows=rows),
        grid=(batch, steps),
        in_specs=[qspec, kvspec, kvspec, _const_spec(bias.shape)],
        out_specs=qspec,
        out_shape=jax.ShapeDtypeStruct(q.shape, BF16),
        compiler_params=_params("parallel", "arbitrary"),
        name="na_attention",
    )(q, k, v, bias)


def _rope_tables(seq):
    t = jnp.arange(seq)
    pos = jnp.stack([t // GRID_W, t % GRID_W], axis=-1).astype(F32)
    n_f = HEAD_DIM // 4
    inv_freq = ROPE_THETA ** (-jnp.arange(n_f, dtype=F32) / n_f)
    ang = pos[:, :, None] * inv_freq
    cos, sin = jnp.cos(ang), jnp.sin(ang)
    cos_h = jnp.concatenate([cos, cos], axis=-1).reshape(seq, HEAD_DIM)
    sin_h = jnp.concatenate([-sin, sin], axis=-1).reshape(seq, HEAD_DIM)
    return jnp.tile(cos_h, (1, LANES // HEAD_DIM)), jnp.tile(sin_h, (1, LANES // HEAD_DIM))


def _norm_rope(x, gain, cos, sin, lo, first_half):
    x2 = x * x
    zero = jnp.zeros_like(x2)
    ms_lo = jnp.sum(jnp.where(lo, x2, zero), axis=-1, keepdims=True) * (1.0 / HEAD_DIM)
    ms_hi = jnp.sum(jnp.where(lo, zero, x2), axis=-1, keepdims=True) * (1.0 / HEAD_DIM)
    r = jnp.where(lo, lax.rsqrt(ms_lo + EPS), lax.rsqrt(ms_hi + EPS))
    y = x * r * gain
    n_f = HEAD_DIM // 4
    partner = jnp.where(first_half, pltpu.roll(y, LANES - n_f, 1), pltpu.roll(y, n_f, 1))
    return y * cos + partner * sin


def _gqa_kernel(q_ref, k_ref, v_ref, cos_ref, sin_ref, qg_ref, kg_ref, o_ref, kn_ref):
    qi = pl.program_id(1)
    seq = k_ref.shape[0]
    lane_k = lax.broadcasted_iota(jnp.int32, (seq, LANES), 1)
    lane_q = lax.broadcasted_iota(jnp.int32, (TQ, LANES), 1)

    @pl.when(qi == 0)
    def _():
        kn = _norm_rope(k_ref[...], kg_ref[...], cos_ref[...], sin_ref[...],
                        lane_k < HEAD_DIM, (lane_k % (HEAD_DIM // 2)) < HEAD_DIM // 4)
        kn_ref[...] = kn.astype(BF16)

    lo = lane_q < HEAD_DIM
    first_half = (lane_q % (HEAD_DIM // 2)) < HEAD_DIM // 4
    qrow = pl.multiple_of(qi * TQ, TQ)
    cos = cos_ref[pl.ds(qrow, TQ), :]
    sin = sin_ref[pl.ds(qrow, TQ), :]
    zero = jnp.zeros((TQ, LANES), F32)
    scale = HEAD_DIM ** -0.5
    heads_per_group = GQA_Q_HEADS // GQA_KV_HEADS
    group_width = heads_per_group * HEAD_DIM

    for g in range(GQA_KV_HEADS):
        stacked = []
        for pair in range(heads_per_group // 2):
            c0 = g * group_width + pair * LANES
            qn = _norm_rope(q_ref[:, c0:c0 + LANES], qg_ref[...], cos, sin, lo, first_half) * scale
            sw = pltpu.roll(qn, HEAD_DIM, 1)
            if g == 0:
                stacked += [jnp.where(lo, qn, zero), jnp.where(lo, sw, zero)]
            else:
                stacked += [jnp.where(lo, zero, sw), jnp.where(lo, zero, qn)]
        qs = jnp.concatenate(stacked, axis=0).astype(BF16)
        s = _dot_nt(qs, kn_ref[...])
        m = jnp.max(s, axis=-1, keepdims=True)
        p = jnp.exp(s - m)
        l = jnp.sum(p, axis=-1, keepdims=True)
        o = _dot(p.astype(BF16), v_ref[...]) / l
        for pair in range(heads_per_group // 2):
            even = o[(2 * pair) * TQ:(2 * pair + 1) * TQ]
            odd = o[(2 * pair + 1) * TQ:(2 * pair + 2) * TQ]
            if g == 0:
                out = jnp.where(lo, even, pltpu.roll(odd, HEAD_DIM, 1))
            else:
                out = jnp.where(lo, pltpu.roll(even, HEAD_DIM, 1), odd)
            c0 = g * group_width + pair * LANES
            o_ref[:, c0:c0 + LANES] = out.astype(BF16)


def _gqa_attention(q, k, v, cos, sin, qg, kg, batch, seq):
    steps = seq // TQ
    qspec = pl.BlockSpec((TQ, GQA_WIDTH), lambda b, i: (b * steps + i, 0))
    kvspec = pl.BlockSpec((seq, KV_WIDTH), lambda b, i: (b, 0))
    return pl.pallas_call(
        _gqa_kernel,
        grid=(batch, steps),
        in_specs=[qspec, kvspec, kvspec, _const_spec((seq, LANES)), _const_spec((seq, LANES)),
                  _const_spec((1, LANES)), _const_spec((1, LANES))],
        out_specs=qspec,
        out_shape=jax.ShapeDtypeStruct(q.shape, BF16),
        scratch_shapes=[pltpu.VMEM((seq, KV_WIDTH), BF16)],
        compiler_params=_params("parallel", "arbitrary"),
        name="gqa_attention",
    )(q, k, v, cos, sin, qg, kg)


def _mem_kv_kernel(m_ref, g_ref, w_ref, o_ref):
    h = _rms(m_ref[...], g_ref[...]).astype(BF16)
    o_ref[...] = _dot(h, w_ref[...]).astype(BF16)


def _mem_kv(mem2, g, w):
    n = mem2.shape[0]
    return pl.pallas_call(
        _mem_kv_kernel,
        grid=(n // TM,),
        in_specs=[pl.BlockSpec((TM, D_MODEL), lambda i: (i, 0)), _const_spec((1, D_MODEL)),
                  _const_spec(w.shape)],
        out_specs=pl.BlockSpec((TM, w.shape[1]), lambda i: (i, 0)),
        out_shape=jax.ShapeDtypeStruct((n, w.shape[1]), BF16),
        compiler_params=_params("parallel"),
        name="mem_kv",
    )(mem2, g, w)


def _post_kernel(x_ref, na_ref, ga_ref, wo_ref, g_ref, wq_ref, kv_ref, wmo_ref, o_ref):
    x1 = x_ref[...] + _dot(na_ref[...], wo_ref[:NA_WIDTH, :]) + _dot(ga_ref[...], wo_ref[NA_WIDTH:, :])
    h = _rms(x1, g_ref[...]).astype(BF16)
    q = (_dot(h, wq_ref[...]) * (MEM_HEAD_DIM ** -0.5)).astype(BF16)
    heads = []
    for hh in range(MEM_HEADS):
        cols = slice(hh * MEM_HEAD_DIM, (hh + 1) * MEM_HEAD_DIM)
        s = _dot_nt(q[:, cols], kv_ref[:, cols])
        m = jnp.max(s, axis=-1, keepdims=True)
        p = jnp.exp(s - m)
        l = jnp.sum(p, axis=-1, keepdims=True)
        vcols = slice(MEM_WIDTH + hh * MEM_HEAD_DIM, MEM_WIDTH + (hh + 1) * MEM_HEAD_DIM)
        heads.append((_dot(p.astype(BF16), kv_ref[:, vcols]) / l).astype(BF16))
    o_ref[...] = x1 + _dot(jnp.concatenate(heads, axis=-1), wmo_ref[...])


def _post(x2, na, ga, wo, g, wq, kv, wmo, seq, mem_len):
    n = x2.shape[0]
    tiles_per_seq = seq // TM
    row = lambda width: pl.BlockSpec((TM, width), lambda i: (i, 0))
    return pl.pallas_call(
        _post_kernel,
        grid=(n // TM,),
        in_specs=[row(D_MODEL), row(NA_WIDTH), row(GQA_WIDTH), _const_spec(wo.shape),
                  _const_spec((1, D_MODEL)), _const_spec(wq.shape),
                  pl.BlockSpec((mem_len, 2 * MEM_WIDTH), lambda i: (i // tiles_per_seq, 0)),
                  _const_spec(wmo.shape)],
        out_specs=row(D_MODEL),
        out_shape=jax.ShapeDtypeStruct((n, D_MODEL), F32),
        compiler_params=_params("parallel"),
        name="post_mixer_memattn",
    )(x2, na, ga, wo, g, wq, kv, wmo)


def _ffn_kernel(xp_ref, x_ref, xn_ref, g_ref, wup_ref, cw_ref, cb_ref, wdn_ref, gf_ref, o_ref,
                *, tiles_per_seq, final_norm):
    i = pl.program_id(0)
    first = (i % tiles_per_seq) == 0
    last = (i % tiles_per_seq) == tiles_per_seq - 1
    x = x_ref[...]
    xa = jnp.concatenate([xp_ref[...], x, xn_ref[...]], axis=0)
    ha = _rms(xa, g_ref[...])
    rid = lax.broadcasted_iota(jnp.int32, (TM + 2 * HALO, 1), 0)
    pad = (first & (rid < HALO)) | (last & (rid >= TM + HALO))
    ha = jnp.where(pad, 0.0, ha).astype(BF16)
    n_all = TM + 2 * HALO

    def conv(u, c0):
        cols = slice(c0, c0 + FF_CHUNK)
        prev = pltpu.roll(u, 1, 0)[HALO:HALO + TM]
        nxt = pltpu.roll(u, n_all - 1, 0)[HALO:HALO + TM]
        return (prev * cw_ref[0:1, cols] + u[HALO:HALO + TM] * cw_ref[1:2, cols]
                + nxt * cw_ref[2:3, cols] + cb_ref[:, cols])

    acc = x
    for c in range(D_FF // FF_CHUNK):
        c0 = c * FF_CHUNK
        gate = conv(_dot(ha, wup_ref[:, c0:c0 + FF_CHUNK]), c0)
        val = conv(_dot(ha, wup_ref[:, D_FF + c0:D_FF + c0 + FF_CHUNK]), D_FF + c0)
        act = (gate * jax.nn.sigmoid(gate) * val).astype(BF16)
        acc = acc + _dot(act, wdn_ref[c0:c0 + FF_CHUNK, :])
    if final_norm:
        acc = _rms(acc, gf_ref[...])
    o_ref[...] = acc


def _ffn(x2, g, wup, cw, cb, wdn, gf, seq, final_norm):
    n = x2.shape[0]
    tiles_per_seq = seq // TM
    hb = TM // HALO
    nh = n // HALO
    return pl.pallas_call(
        functools.partial(_ffn_kernel, tiles_per_seq=tiles_per_seq, final_norm=final_norm),
        grid=(n // TM,),
        in_specs=[pl.BlockSpec((HALO, D_MODEL), lambda i: (jnp.maximum(i * hb - 1, 0), 0)),
                  pl.BlockSpec((TM, D_MODEL), lambda i: (i, 0)),
                  pl.BlockSpec((HALO, D_MODEL), lambda i: (jnp.minimum((i + 1) * hb, nh - 1), 0)),
                  _const_spec((1, D_MODEL)), _const_spec(wup.shape), _const_spec(cw.shape),
                  _const_spec(cb.shape), _const_spec(wdn.shape), _const_spec((1, D_MODEL))],
        out_specs=pl.BlockSpec((TM, D_MODEL), lambda i: (i, 0)),
        out_shape=jax.ShapeDtypeStruct((n, D_MODEL), F32),
        compiler_params=_params("parallel"),
        name="conv_ffn",
    )(x2, x2, x2, g, wup, cw, cb, wdn, gf)


def kernel(x, mem, norm_mix, w_in, na_rpb, gqa_q_norm, gqa_k_norm, w_out, norm_mem_q, norm_mem_kv,
           w_mem_q, w_mem_kv, w_mem_o, norm_ffn, w_up, conv_w, conv_b, w_down, norm_final):
    batch, seq, d = x.shape
    depth = w_in.shape[0]
    mem_len = mem.shape[1]
    rows = seq // GRID_W
    assert d == D_MODEL and seq % GRID_W == 0 and rows >= NA_KH and rows % NA_ROWS_PER_STEP == 0
    assert seq % TM == 0 and seq % TQ == 0 and (batch * mem_len) % TM == 0

    cos, sin = _rope_tables(seq)
    x2 = x.reshape(batch * seq, d)
    mem2 = mem.reshape(batch * mem_len, d)
    row = lambda a: a.reshape(1, -1)
    lanes2 = lambda a: jnp.tile(a, LANES // HEAD_DIM).reshape(1, LANES)

    for l in range(depth):
        naq, nak, nav, gq, gk, gv = _in_proj(x2, row(norm_mix[l]), w_in[l].astype(BF16))
        na = _na_attention(naq, nak, nav, _na_bias_table(na_rpb[l], rows), batch, seq)
        ga = _gqa_attention(gq, gk, gv, cos, sin, lanes2(gqa_q_norm[l]), lanes2(gqa_k_norm[l]), batch, seq)
        kv = _mem_kv(mem2, row(norm_mem_kv[l]), w_mem_kv[l].astype(BF16))
        x2 = _post(x2, na, ga, w_out[l].astype(BF16), row(norm_mem_q[l]), w_mem_q[l].astype(BF16), kv,
                   w_mem_o[l].astype(BF16), seq, mem_len)
        x2 = _ffn(x2, row(norm_ffn[l]), w_up[l].astype(BF16), conv_w[l], row(conv_b[l]),
                  w_down[l].astype(BF16), row(norm_final), seq, final_norm=(l == depth - 1))
    return x2.reshape(batch, seq, d)
```

```python
import functools

import jax
import jax.numpy as jnp
from jax import lax
from jax.experimental import pallas as pl
from jax.experimental.pallas import tpu as pltpu

D_MODEL = 1024
GRID_W = 64
HEAD_DIM = 64
NA_HEADS = 8
GQA_Q_HEADS = 8
GQA_KV_HEADS = 2
NA_KH = 8
NA_KW = 16
ROPE_THETA = 10000.0
NA_WIDTH = NA_HEADS * HEAD_DIM
GQA_WIDTH = GQA_Q_HEADS * HEAD_DIM
KV_WIDTH = GQA_KV_HEADS * HEAD_DIM
MEM_HEADS = 4
MEM_HEAD_DIM = 128
MEM_WIDTH = MEM_HEADS * MEM_HEAD_DIM
D_FF = 2816
EPS = 1e-6
NEG_INF = -1e30

LANES = 128
VMEM_LIMIT = 56 * 1024 * 1024

TM = 512
HALO = 16
FF_CHUNK = 1408
GQA_TQ = 1024
GQA_UNIT = 64
NA_ROWS_PER_STEP = 8
LOG2_E = 1.4426950408889634

BF16 = jnp.bfloat16
F32 = jnp.float32


def _rms(x, g):
    return x * lax.rsqrt(jnp.mean(x * x, axis=-1, keepdims=True) + EPS) * g


def _dot(a, b):
    return jnp.dot(a, b, preferred_element_type=F32)


def _dot_nt(a, b):
    return lax.dot_general(a, b, (((1,), (1,)), ((), ())), preferred_element_type=F32)


def _const_spec(shape):
    nd = len(shape)
    return pl.BlockSpec(shape, lambda *_: (0,) * nd, pipeline_mode=pl.Buffered(1))


def _params(*sem):
    return pltpu.CompilerParams(dimension_semantics=sem, vmem_limit_bytes=VMEM_LIMIT)


def _in_proj_kernel(x_ref, g_ref, w_ref, naq_ref, nak_ref, nav_ref, gq_ref, gk_ref, gv_ref):
    h = _rms(x_ref[...], g_ref[...]).astype(BF16)
    o = 0
    naq_ref[...] = (_dot(h, w_ref[:, o:o + NA_WIDTH]) * (HEAD_DIM ** -0.5)).astype(BF16)
    o += NA_WIDTH
    nak_ref[...] = _dot(h, w_ref[:, o:o + NA_WIDTH]).astype(BF16)
    o += NA_WIDTH
    nav_ref[...] = _dot(h, w_ref[:, o:o + NA_WIDTH]).astype(BF16)
    o += NA_WIDTH
    gq_ref[...] = _dot(h, w_ref[:, o:o + GQA_WIDTH])
    o += GQA_WIDTH
    gk_ref[...] = _dot(h, w_ref[:, o:o + KV_WIDTH])
    o += KV_WIDTH
    gv_ref[...] = _dot(h, w_ref[:, o:o + KV_WIDTH]).astype(BF16)


def _in_proj(x2, g, w):
    n = x2.shape[0]
    in_width = w.shape[1]
    row = lambda width: pl.BlockSpec((TM, width), lambda i: (i, 0))
    return pl.pallas_call(
        _in_proj_kernel,
        grid=(n // TM,),
        in_specs=[row(D_MODEL), _const_spec((1, D_MODEL)), _const_spec((D_MODEL, in_width))],
        out_specs=[row(NA_WIDTH), row(NA_WIDTH), row(NA_WIDTH), row(GQA_WIDTH), row(KV_WIDTH), row(KV_WIDTH)],
        out_shape=[
            jax.ShapeDtypeStruct((n, NA_WIDTH), BF16),
            jax.ShapeDtypeStruct((n, NA_WIDTH), BF16),
            jax.ShapeDtypeStruct((n, NA_WIDTH), BF16),
            jax.ShapeDtypeStruct((n, GQA_WIDTH), F32),
            jax.ShapeDtypeStruct((n, KV_WIDTH), F32),
            jax.ShapeDtypeStruct((n, KV_WIDTH), BF16),
        ],
        compiler_params=_params("parallel"),
        name="in_proj",
    )(x2, g, w)


N_ROW_OFF = 2 * NA_KH - 1
N_COL_OFF = 2 * NA_KW - 1


def _na_bias_kernel(rpb_ref, o_ref, t_ref):
    cls = pl.program_id(0)
    lane = lax.broadcasted_iota(jnp.int32, (GRID_W, LANES), 1)

    @pl.when(cls == 0)
    def _():
        q = lax.broadcasted_iota(jnp.int32, (GRID_W, LANES), 0)
        k = lane % GRID_W
        col_off = jnp.clip(k - q, -(NA_KW - 1), NA_KW - 1) + (NA_KW - 1)
        col_start = jnp.clip(q - NA_KW // 2, 0, GRID_W - NA_KW)
        valid = (k >= col_start) & (k < col_start + NA_KW)

        def block(b, carry):
            t = jnp.full((GRID_W, LANES), NEG_INF, F32)
            for d in range(N_COL_OFF):
                t = jnp.where(valid & (col_off == d), rpb_ref[b * N_COL_OFF + d], t)
            t_ref[b] = t
            return carry

        lax.fori_loop(0, NA_HEADS * N_ROW_OFF, block, 0)

    lo = lane < GRID_W
    for h in range(NA_HEADS):
        for ip in range(NA_KH // 2):
            ro = 2 * ip - cls + (NA_KH - 1)
            blk = jnp.where(lo, t_ref[h * N_ROW_OFF + ro], t_ref[h * N_ROW_OFF + ro + 1])
            o_ref[0, h // 2, (h % 2) * GRID_W:(h % 2 + 1) * GRID_W, ip * LANES:(ip + 1) * LANES] = blk


def _na_bias_table(rpb):
    shape = (NA_KH, NA_HEADS // 2, 2 * GRID_W, NA_KH * GRID_W)
    return pl.pallas_call(
        _na_bias_kernel,
        grid=(NA_KH,),
        in_specs=[pl.BlockSpec(memory_space=pltpu.SMEM)],
        out_specs=pl.BlockSpec((1,) + shape[1:], lambda c: (c, 0, 0, 0)),
        out_shape=jax.ShapeDtypeStruct(shape, F32),
        scratch_shapes=[pltpu.VMEM((NA_HEADS * N_ROW_OFF, GRID_W, LANES), F32)],
        compiler_params=_params("arbitrary"),
        name="na_bias_table",
    )(rpb.reshape(-1))


def _na_kernel(q_ref, k_ref, v_ref, bias_ref, o_ref, *, rows):
    rb = pl.program_id(1)
    lo = lax.broadcasted_iota(jnp.int32, (GRID_W, LANES), 1) < HEAD_DIM
    win = NA_KH * GRID_W

    def body(i, carry):
        r = rb * NA_ROWS_PER_STEP + i
        row_start = jnp.clip(r - NA_KH // 2, 0, rows - NA_KH)
        cls = r - row_start
        kstart = pl.multiple_of(row_start * GRID_W, GRID_W)
        qstart = pl.multiple_of(i * GRID_W, GRID_W)
        for j in range(NA_HEADS // 2):
            cols = slice(j * LANES, (j + 1) * LANES)
            qp = q_ref[pl.ds(qstart, GRID_W), cols]
            zero = jnp.zeros_like(qp)
            qs = jnp.concatenate([jnp.where(lo, qp, zero), jnp.where(lo, zero, qp)], axis=0)
            kp = k_ref[pl.ds(kstart, win), cols]
            s = _dot_nt(qs, kp) + bias_ref[cls, j]
            m = jnp.max(s, axis=-1, keepdims=True)
            p = jnp.exp(s - m)
            l = jnp.sum(p, axis=-1, keepdims=True)
            vp = v_ref[pl.ds(kstart, win), cols]
            o = _dot(p.astype(BF16), vp) / l
            o_ref[pl.ds(qstart, GRID_W), cols] = jnp.where(lo, o[:GRID_W], o[GRID_W:]).astype(BF16)
        return carry

    lax.fori_loop(0, NA_ROWS_PER_STEP, body, 0)


def _na_attention(q, k, v, bias, batch, seq):
    rows = seq // GRID_W
    tokens = NA_ROWS_PER_STEP * GRID_W
    steps = rows // NA_ROWS_PER_STEP
    qspec = pl.BlockSpec((tokens, NA_WIDTH), lambda b, r: (b * steps + r, 0))
    kvspec = pl.BlockSpec((seq, NA_WIDTH), lambda b, r: (b, 0))
    return pl.pallas_call(
        functools.partial(_na_kernel, rows=rows),
        grid=(batch, steps),
        in_specs=[qspec, kvspec, kvspec, _const_spec(bias.shape)],
        out_specs=qspec,
        out_shape=jax.ShapeDtypeStruct(q.shape, BF16),
        compiler_params=_params("parallel", "arbitrary"),
        name="na_attention",
    )(q, k, v, bias)


def _rope_tables(seq):
    t = jnp.arange(seq)
    pos = jnp.stack([t // GRID_W, t % GRID_W], axis=-1).astype(F32)
    n_f = HEAD_DIM // 4
    inv_freq = ROPE_THETA ** (-jnp.arange(n_f, dtype=F32) / n_f)
    ang = pos[:, :, None] * inv_freq
    cos, sin = jnp.cos(ang), jnp.sin(ang)
    cos_h = jnp.concatenate([cos, cos], axis=-1).reshape(seq, HEAD_DIM)
    sin_h = jnp.concatenate([-sin, sin], axis=-1).reshape(seq, HEAD_DIM)
    return jnp.tile(cos_h, (1, LANES // HEAD_DIM)), jnp.tile(sin_h, (1, LANES // HEAD_DIM))


def _norm_rope(x, gain, cos, sin, lo, first_half):
    x2 = x * x
    zero = jnp.zeros_like(x2)
    ms_lo = jnp.sum(jnp.where(lo, x2, zero), axis=-1, keepdims=True) * (1.0 / HEAD_DIM)
    ms_hi = jnp.sum(jnp.where(lo, zero, x2), axis=-1, keepdims=True) * (1.0 / HEAD_DIM)
    r = jnp.where(lo, lax.rsqrt(ms_lo + EPS), lax.rsqrt(ms_hi + EPS))
    y = x * r * gain
    n_f = HEAD_DIM // 4
    partner = jnp.where(first_half, pltpu.roll(y, LANES - n_f, 1), pltpu.roll(y, n_f, 1))
    return y * cos + partner * sin


def _gqa_kernel(q_ref, k_ref, v_ref, cos_ref, sin_ref, qg_ref, kg_ref, o_ref, kn_ref, vx_ref,
                s0_ref, s1_ref, m0_ref, m1_ref):
    qi = pl.program_id(1)
    seq = k_ref.shape[0]
    n_units = GQA_TQ // GQA_UNIT
    heads_per_group = GQA_Q_HEADS // GQA_KV_HEADS
    s_refs = (s0_ref, s1_ref)
    m_refs = (m0_ref, m1_ref)

    @pl.when(qi == 0)
    def _():
        lane_k = lax.broadcasted_iota(jnp.int32, (seq, LANES), 1)
        kn = _norm_rope(k_ref[...], kg_ref[...], cos_ref[...], sin_ref[...],
                        lane_k < HEAD_DIM, (lane_k % (HEAD_DIM // 2)) < HEAD_DIM // 4)
        kn_ref[...] = kn.astype(BF16)
        vx_ref[:, :KV_WIDTH] = v_ref[...]
        vx_ref[:, KV_WIDTH:] = jnp.ones((seq, LANES), BF16)

    lane = lax.broadcasted_iota(jnp.int32, (GQA_UNIT, LANES), 1)
    lo = lane < HEAD_DIM
    first_half = (lane % (HEAD_DIM // 2)) < HEAD_DIM // 4
    zero = jnp.zeros((GQA_UNIT, LANES), F32)
    scale = HEAD_DIM ** -0.5 * LOG2_E

    def scores(u, slot):
        r0 = pl.multiple_of(u * GQA_UNIT, GQA_UNIT)
        t0 = pl.multiple_of(qi * GQA_TQ + u * GQA_UNIT, GQA_UNIT)
        cos = cos_ref[pl.ds(t0, GQA_UNIT), :]
        sin = sin_ref[pl.ds(t0, GQA_UNIT), :]
        stacked = []
        for pair in range(GQA_Q_HEADS // 2):
            qp = q_ref[pl.ds(r0, GQA_UNIT), pair * LANES:(pair + 1) * LANES]
            qn = _norm_rope(qp, qg_ref[...], cos, sin, lo, first_half) * scale
            sw = pltpu.roll(qn, HEAD_DIM, 1)
            if pair < heads_per_group // 2:
                stacked += [jnp.where(lo, qn, zero), jnp.where(lo, sw, zero)]
            else:
                stacked += [jnp.where(lo, zero, sw), jnp.where(lo, zero, qn)]
        qs = jnp.concatenate(stacked, axis=0).astype(BF16)
        s = _dot_nt(qs, kn_ref[...])
        s_refs[slot][...] = s
        m_refs[slot][...] = jnp.broadcast_to(jnp.max(s, axis=-1, keepdims=True),
                                             (GQA_Q_HEADS * GQA_UNIT, LANES))

    def finish(u, slot):
        p = jnp.exp2(s_refs[slot][...] - m_refs[slot][:, :1]).astype(BF16)
        o = _dot(p, vx_ref[...])
        o = o[:, :LANES] / o[:, LANES:]
        r0 = pl.multiple_of(u * GQA_UNIT, GQA_UNIT)
        for pair in range(GQA_Q_HEADS // 2):
            even = o[(2 * pair) * GQA_UNIT:(2 * pair + 1) * GQA_UNIT]
            odd = o[(2 * pair + 1) * GQA_UNIT:(2 * pair + 2) * GQA_UNIT]
            if pair < heads_per_group // 2:
                out = jnp.where(lo, even, pltpu.roll(odd, HEAD_DIM, 1))
            else:
                out = jnp.where(lo, pltpu.roll(even, HEAD_DIM, 1), odd)
            o_ref[pl.ds(r0, GQA_UNIT), pair * LANES:(pair + 1) * LANES] = out.astype(BF16)

    scores(0, 0)

    def body(i, carry):
        scores(2 * i + 1, 1)
        finish(2 * i, 0)
        scores(jnp.minimum(2 * i + 2, n_units - 1), 0)
        finish(2 * i + 1, 1)
        return carry

    lax.fori_loop(0, n_units // 2, body, 0)


def _gqa_attention(q, k, v, cos, sin, qg, kg, batch, seq):
    steps = seq // GQA_TQ
    qspec = pl.BlockSpec((GQA_TQ, GQA_WIDTH), lambda b, i: (b * steps + i, 0))
    kvspec = pl.BlockSpec((seq, KV_WIDTH), lambda b, i: (b, 0))
    stacked_rows = GQA_Q_HEADS * GQA_UNIT
    return pl.pallas_call(
        _gqa_kernel,
        grid=(batch, steps),
        in_specs=[qspec, kvspec, kvspec, _const_spec((seq, LANES)), _const_spec((seq, LANES)),
                  _const_spec((1, LANES)), _const_spec((1, LANES))],
        out_specs=qspec,
        out_shape=jax.ShapeDtypeStruct(q.shape, BF16),
        scratch_shapes=[pltpu.VMEM((seq, KV_WIDTH), BF16), pltpu.VMEM((seq, KV_WIDTH + LANES), BF16),
                        pltpu.VMEM((stacked_rows, seq), F32), pltpu.VMEM((stacked_rows, seq), F32),
                        pltpu.VMEM((stacked_rows, LANES), F32), pltpu.VMEM((stacked_rows, LANES), F32)],
        compiler_params=_params("parallel", "arbitrary"),
        name="gqa_attention",
    )(q, k, v, cos, sin, qg, kg)


def _mem_kv_kernel(m_ref, g_ref, w_ref, o_ref):
    h = _rms(m_ref[...], g_ref[...]).astype(BF16)
    o_ref[...] = _dot(h, w_ref[...]).astype(BF16)


def _mem_kv(mem2, g, w):
    n = mem2.shape[0]
    return pl.pallas_call(
        _mem_kv_kernel,
        grid=(n // TM,),
        in_specs=[pl.BlockSpec((TM, D_MODEL), lambda i: (i, 0)), _const_spec((1, D_MODEL)),
                  _const_spec(w.shape)],
        out_specs=pl.BlockSpec((TM, w.shape[1]), lambda i: (i, 0)),
        out_shape=jax.ShapeDtypeStruct((n, w.shape[1]), BF16),
        compiler_params=_params("parallel"),
        name="mem_kv",
    )(mem2, g, w)


def _post_kernel(x_ref, na_ref, ga_ref, wo_ref, g_ref, wq_ref, kv_ref, wmo_ref, o_ref):
    x1 = x_ref[...] + _dot(na_ref[...], wo_ref[:NA_WIDTH, :]) + _dot(ga_ref[...], wo_ref[NA_WIDTH:, :])
    h = _rms(x1, g_ref[...]).astype(BF16)
    q = (_dot(h, wq_ref[...]) * (MEM_HEAD_DIM ** -0.5)).astype(BF16)
    heads = []
    for hh in range(MEM_HEADS):
        cols = slice(hh * MEM_HEAD_DIM, (hh + 1) * MEM_HEAD_DIM)
        s = _dot_nt(q[:, cols], kv_ref[:, cols])
        m = jnp.max(s, axis=-1, keepdims=True)
        p = jnp.exp(s - m)
        l = jnp.sum(p, axis=-1, keepdims=True)
        vcols = slice(MEM_WIDTH + hh * MEM_HEAD_DIM, MEM_WIDTH + (hh + 1) * MEM_HEAD_DIM)
        heads.append((_dot(p.astype(BF16), kv_ref[:, vcols]) / l).astype(BF16))
    o_ref[...] = x1 + _dot(jnp.concatenate(heads, axis=-1), wmo_ref[...])


def _post(x2, na, ga, wo, g, wq, kv, wmo, seq, mem_len):
    n = x2.shape[0]
    tiles_per_seq = seq // TM
    row = lambda width: pl.BlockSpec((TM, width), lambda i: (i, 0))
    return pl.pallas_call(
        _post_kernel,
        grid=(n // TM,),
        in_specs=[row(D_MODEL), row(NA_WIDTH), row(GQA_WIDTH), _const_spec(wo.shape),
                  _const_spec((1, D_MODEL)), _const_spec(wq.shape),
                  pl.BlockSpec((mem_len, 2 * MEM_WIDTH), lambda i: (i // tiles_per_seq, 0)),
                  _const_spec(wmo.shape)],
        out_specs=row(D_MODEL),
        out_shape=jax.ShapeDtypeStruct((n, D_MODEL), F32),
        compiler_params=_params("parallel"),
        name="post_mixer_memattn",
    )(x2, na, ga, wo, g, wq, kv, wmo)


def _ffn_kernel(xp_ref, x_ref, xn_ref, g_ref, wup_ref, cw_ref, cb_ref, wdn_ref, gf_ref, o_ref,
                *, tiles_per_seq, final_norm):
    i = pl.program_id(0)
    first = (i % tiles_per_seq) == 0
    last = (i % tiles_per_seq) == tiles_per_seq - 1
    x = x_ref[...]
    xa = jnp.concatenate([xp_ref[...], x, xn_ref[...]], axis=0)
    ha = _rms(xa, g_ref[...])
    rid = lax.broadcasted_iota(jnp.int32, (TM + 2 * HALO, 1), 0)
    pad = (first & (rid < HALO)) | (last & (rid >= TM + HALO))
    ha = jnp.where(pad, 0.0, ha).astype(BF16)
    n_all = TM + 2 * HALO

    def conv(u, c0):
        cols = slice(c0, c0 + FF_CHUNK)
        prev = pltpu.roll(u, 1, 0)[HALO:HALO + TM]
        nxt = pltpu.roll(u, n_all - 1, 0)[HALO:HALO + TM]
        return (prev * cw_ref[0:1, cols] + u[HALO:HALO + TM] * cw_ref[1:2, cols]
                + nxt * cw_ref[2:3, cols] + cb_ref[:, cols])

    acc = x
    for c in range(D_FF // FF_CHUNK):
        c0 = c * FF_CHUNK
        gate = conv(_dot(ha, wup_ref[:, c0:c0 + FF_CHUNK]), c0)
        val = conv(_dot(ha, wup_ref[:, D_FF + c0:D_FF + c0 + FF_CHUNK]), D_FF + c0)
        act = (gate * jax.nn.sigmoid(gate) * val).astype(BF16)
        acc = acc + _dot(act, wdn_ref[c0:c0 + FF_CHUNK, :])
    if final_norm:
        acc = _rms(acc, gf_ref[...])
    o_ref[...] = acc


def _ffn(x2, g, wup, cw, cb, wdn, gf, seq, final_norm):
    n = x2.shape[0]
    tiles_per_seq = seq // TM
    hb = TM // HALO
    nh = n // HALO
    return pl.pallas_call(
        functools.partial(_ffn_kernel, tiles_per_seq=tiles_per_seq, final_norm=final_norm),
        grid=(n // TM,),
        in_specs=[pl.BlockSpec((HALO, D_MODEL), lambda i: (jnp.maximum(i * hb - 1, 0), 0)),
                  pl.BlockSpec((TM, D_MODEL), lambda i: (i, 0)),
                  pl.BlockSpec((HALO, D_MODEL), lambda i: (jnp.minimum((i + 1) * hb, nh - 1), 0)),
                  _const_spec((1, D_MODEL)), _const_spec(wup.shape), _const_spec(cw.shape),
                  _const_spec(cb.shape), _const_spec(wdn.shape), _const_spec((1, D_MODEL))],
        out_specs=pl.BlockSpec((TM, D_MODEL), lambda i: (i, 0)),
        out_shape=jax.ShapeDtypeStruct((n, D_MODEL), F32),
        compiler_params=_params("parallel"),
        name="conv_ffn",
    )(x2, x2, x2, g, wup, cw, cb, wdn, gf)


def kernel(x, mem, norm_mix, w_in, na_rpb, gqa_q_norm, gqa_k_norm, w_out, norm_mem_q, norm_mem_kv,
           w_mem_q, w_mem_kv, w_mem_o, norm_ffn, w_up, conv_w, conv_b, w_down, norm_final):
    batch, seq, d = x.shape
    depth = w_in.shape[0]
    mem_len = mem.shape[1]
    rows = seq // GRID_W
    assert d == D_MODEL and seq % GRID_W == 0 and rows >= NA_KH and rows % NA_ROWS_PER_STEP == 0
    assert seq % TM == 0 and seq % GQA_TQ == 0 and (batch * mem_len) % TM == 0

    cos, sin = _rope_tables(seq)
    x2 = x.reshape(batch * seq, d)
    mem2 = mem.reshape(batch * mem_len, d)
    row = lambda a: a.reshape(1, -1)
    lanes2 = lambda a: jnp.tile(a, LANES // HEAD_DIM).reshape(1, LANES)

    for l in range(depth):
        naq, nak, nav, gq, gk, gv = _in_proj(x2, row(norm_mix[l]), w_in[l].astype(BF16))
        na = _na_attention(naq, nak, nav, _na_bias_table(na_rpb[l]), batch, seq)
        ga = _gqa_attention(gq, gk, gv, cos, sin, lanes2(gqa_q_norm[l]), lanes2(gqa_k_norm[l]), batch, seq)
        kv = _mem_kv(mem2, row(norm_mem_kv[l]), w_mem_kv[l].astype(BF16))
        x2 = _post(x2, na, ga, w_out[l].astype(BF16), row(norm_mem_q[l]), w_mem_q[l].astype(BF16), kv,
                   w_mem_o[l].astype(BF16), seq, mem_len)
        x2 = _ffn(x2, row(norm_ffn[l]), w_up[l].astype(BF16), conv_w[l], row(conv_b[l]),
                  w_down[l].astype(BF16), row(norm_final), seq, final_norm=(l == depth - 1))
    return x2.reshape(batch, seq, d)
```

```python
import functools

import jax
import jax.numpy as jnp
from jax import lax
from jax.experimental import pallas as pl
from jax.experimental.pallas import tpu as pltpu

D_MODEL = 1024
GRID_W = 64
HEAD_DIM = 64
NA_HEADS = 8
GQA_Q_HEADS = 8
GQA_KV_HEADS = 2
NA_KH = 8
NA_KW = 16
ROPE_THETA = 10000.0
NA_WIDTH = NA_HEADS * HEAD_DIM
GQA_WIDTH = GQA_Q_HEADS * HEAD_DIM
KV_WIDTH = GQA_KV_HEADS * HEAD_DIM
MEM_HEADS = 4
MEM_HEAD_DIM = 128
MEM_WIDTH = MEM_HEADS * MEM_HEAD_DIM
D_FF = 2816
EPS = 1e-6
NEG_INF = -1e30

LANES = 128
VMEM_LIMIT = 56 * 1024 * 1024

TM = 512
HALO = 8
FF_CHUNK = 1408
GQA_TQ = 1024
GQA_UNIT = 64
NA_ROWS_PER_STEP = 8
LOG2_E = 1.4426950408889634

BF16 = jnp.bfloat16
F32 = jnp.float32


def _rms(x, g):
    return x * lax.rsqrt(jnp.mean(x * x, axis=-1, keepdims=True) + EPS) * g


def _dot(a, b):
    return jnp.dot(a, b, preferred_element_type=F32)


def _dot_nt(a, b):
    return lax.dot_general(a, b, (((1,), (1,)), ((), ())), preferred_element_type=F32)


def _const_spec(shape):
    nd = len(shape)
    return pl.BlockSpec(shape, lambda *_: (0,) * nd, pipeline_mode=pl.Buffered(1))


def _params(*sem):
    return pltpu.CompilerParams(dimension_semantics=sem, vmem_limit_bytes=VMEM_LIMIT)


def _in_proj_kernel(x_ref, g_ref, w_ref, naq_ref, nak_ref, nav_ref, gq_ref, gk_ref, gv_ref):
    h = _rms(x_ref[...], g_ref[...]).astype(BF16)
    o = 0
    naq_ref[...] = (_dot(h, w_ref[:, o:o + NA_WIDTH]) * (HEAD_DIM ** -0.5 * LOG2_E)).astype(BF16)
    o += NA_WIDTH
    nak_ref[...] = _dot(h, w_ref[:, o:o + NA_WIDTH]).astype(BF16)
    o += NA_WIDTH
    nav_ref[...] = _dot(h, w_ref[:, o:o + NA_WIDTH]).astype(BF16)
    o += NA_WIDTH
    gq_ref[...] = _dot(h, w_ref[:, o:o + GQA_WIDTH])
    o += GQA_WIDTH
    gk_ref[...] = _dot(h, w_ref[:, o:o + KV_WIDTH])
    o += KV_WIDTH
    gv_ref[...] = _dot(h, w_ref[:, o:o + KV_WIDTH]).astype(BF16)


def _in_proj(x2, g, w):
    n = x2.shape[0]
    in_width = w.shape[1]
    row = lambda width: pl.BlockSpec((TM, width), lambda i: (i, 0))
    return pl.pallas_call(
        _in_proj_kernel,
        grid=(n // TM,),
        in_specs=[row(D_MODEL), _const_spec((1, D_MODEL)), _const_spec((D_MODEL, in_width))],
        out_specs=[row(NA_WIDTH), row(NA_WIDTH), row(NA_WIDTH), row(GQA_WIDTH), row(KV_WIDTH), row(KV_WIDTH)],
        out_shape=[
            jax.ShapeDtypeStruct((n, NA_WIDTH), BF16),
            jax.ShapeDtypeStruct((n, NA_WIDTH), BF16),
            jax.ShapeDtypeStruct((n, NA_WIDTH), BF16),
            jax.ShapeDtypeStruct((n, GQA_WIDTH), F32),
            jax.ShapeDtypeStruct((n, KV_WIDTH), F32),
            jax.ShapeDtypeStruct((n, KV_WIDTH), BF16),
        ],
        compiler_params=_params("parallel"),
        name="in_proj",
    )(x2, g, w)


N_ROW_OFF = 2 * NA_KH - 1
N_COL_OFF = 2 * NA_KW - 1


def _na_bias_kernel(rpb_ref, o_ref, t_ref):
    cls = pl.program_id(0)
    lane = lax.broadcasted_iota(jnp.int32, (GRID_W, LANES), 1)

    @pl.when(cls == 0)
    def _():
        q = lax.broadcasted_iota(jnp.int32, (GRID_W, LANES), 0)
        k = lane % GRID_W
        col_off = jnp.clip(k - q, -(NA_KW - 1), NA_KW - 1) + (NA_KW - 1)
        col_start = jnp.clip(q - NA_KW // 2, 0, GRID_W - NA_KW)
        valid = (k >= col_start) & (k < col_start + NA_KW)

        def block(b, carry):
            t = jnp.full((GRID_W, LANES), NEG_INF, F32)
            for d in range(N_COL_OFF):
                t = jnp.where(valid & (col_off == d), rpb_ref[b * N_COL_OFF + d], t)
            t_ref[b] = t * LOG2_E
            return carry

        lax.fori_loop(0, NA_HEADS * N_ROW_OFF, block, 0)

    lo = lane < GRID_W
    for h in range(NA_HEADS):
        for ip in range(NA_KH // 2):
            ro = 2 * ip - cls + (NA_KH - 1)
            blk = jnp.where(lo, t_ref[h * N_ROW_OFF + ro], t_ref[h * N_ROW_OFF + ro + 1])
            o_ref[0, h // 2, (h % 2) * GRID_W:(h % 2 + 1) * GRID_W, ip * LANES:(ip + 1) * LANES] = blk


def _na_bias_table(rpb):
    shape = (NA_KH, NA_HEADS // 2, 2 * GRID_W, NA_KH * GRID_W)
    return pl.pallas_call(
        _na_bias_kernel,
        grid=(NA_KH,),
        in_specs=[pl.BlockSpec(memory_space=pltpu.SMEM)],
        out_specs=pl.BlockSpec((1,) + shape[1:], lambda c: (c, 0, 0, 0)),
        out_shape=jax.ShapeDtypeStruct(shape, F32),
        scratch_shapes=[pltpu.VMEM((NA_HEADS * N_ROW_OFF, GRID_W, LANES), F32)],
        compiler_params=_params("arbitrary"),
        name="na_bias_table",
    )(rpb.reshape(-1))


def _na_kernel(q_ref, k_ref, v_ref, bias_ref, o_ref, *, rows):
    rb = pl.program_id(1)
    lo = lax.broadcasted_iota(jnp.int32, (GRID_W, LANES), 1) < HEAD_DIM
    win = NA_KH * GRID_W
    pairs = [slice(j * LANES, (j + 1) * LANES) for j in range(NA_HEADS // 2)]
    ones = jnp.ones((win, LANES), BF16)

    def window(i):
        r = rb * NA_ROWS_PER_STEP + i
        row_start = jnp.clip(r - NA_KH // 2, 0, rows - NA_KH)
        return r - row_start, pl.multiple_of(row_start * GRID_W, GRID_W)

    def scores(i):
        cls, kstart = window(i)
        out = []
        for j, cols in enumerate(pairs):
            qp = q_ref[i * GRID_W:(i + 1) * GRID_W, cols]
            zero = jnp.zeros_like(qp)
            qs = jnp.concatenate([jnp.where(lo, qp, zero), jnp.where(lo, zero, qp)], axis=0)
            out.append(_dot_nt(qs, k_ref[pl.ds(kstart, win), cols]) + bias_ref[cls, j])
        return out

    def finish(i, s_list):
        _, kstart = window(i)
        p_list = [jnp.exp2(s - jnp.max(s, axis=-1, keepdims=True)).astype(BF16) for s in s_list]
        for p, cols in zip(p_list, pairs):
            o = _dot(p, jnp.concatenate([v_ref[pl.ds(kstart, win), cols], ones], axis=1))
            o = o[:, :LANES] / o[:, LANES:]
            o_ref[i * GRID_W:(i + 1) * GRID_W, cols] = jnp.where(lo, o[:GRID_W], o[GRID_W:]).astype(BF16)

    s_next = scores(0)
    for i in range(NA_ROWS_PER_STEP):
        s_cur = s_next
        if i + 1 < NA_ROWS_PER_STEP:
            s_next = scores(i + 1)
        finish(i, s_cur)


def _na_attention(q, k, v, bias, batch, seq):
    rows = seq // GRID_W
    tokens = NA_ROWS_PER_STEP * GRID_W
    steps = rows // NA_ROWS_PER_STEP
    qspec = pl.BlockSpec((tokens, NA_WIDTH), lambda b, r: (b * steps + r, 0))
    kvspec = pl.BlockSpec((seq, NA_WIDTH), lambda b, r: (b, 0))
    return pl.pallas_call(
        functools.partial(_na_kernel, rows=rows),
        grid=(batch, steps),
        in_specs=[qspec, kvspec, kvspec, _const_spec(bias.shape)],
        out_specs=qspec,
        out_shape=jax.ShapeDtypeStruct(q.shape, BF16),
        compiler_params=_params("parallel", "arbitrary"),
        name="na_attention",
    )(q, k, v, bias)


def _rope_tables(seq):
    t = jnp.arange(seq)
    pos = jnp.stack([t // GRID_W, t % GRID_W], axis=-1).astype(F32)
    n_f = HEAD_DIM // 4
    inv_freq = ROPE_THETA ** (-jnp.arange(n_f, dtype=F32) / n_f)
    ang = pos[:, :, None] * inv_freq
    cos, sin = jnp.cos(ang), jnp.sin(ang)
    cos_h = jnp.concatenate([cos, cos], axis=-1).reshape(seq, HEAD_DIM)
    sin_h = jnp.concatenate([-sin, sin], axis=-1).reshape(seq, HEAD_DIM)
    return jnp.tile(cos_h, (1, LANES // HEAD_DIM)), jnp.tile(sin_h, (1, LANES // HEAD_DIM))


def _norm_rope(x, gain, cos, sin, lo, first_half):
    x2 = x * x
    zero = jnp.zeros_like(x2)
    ms_lo = jnp.sum(jnp.where(lo, x2, zero), axis=-1, keepdims=True) * (1.0 / HEAD_DIM)
    ms_hi = jnp.sum(jnp.where(lo, zero, x2), axis=-1, keepdims=True) * (1.0 / HEAD_DIM)
    r = jnp.where(lo, lax.rsqrt(ms_lo + EPS), lax.rsqrt(ms_hi + EPS))
    y = x * r * gain
    n_f = HEAD_DIM // 4
    partner = jnp.where(first_half, pltpu.roll(y, LANES - n_f, 1), pltpu.roll(y, n_f, 1))
    return y * cos + partner * sin


def _gqa_kernel(q_ref, k_ref, v_ref, cos_ref, sin_ref, qg_ref, kg_ref, o_ref, kn_ref, vx_ref,
                q0_ref, q1_ref, s0_ref, s1_ref, m0_ref, m1_ref):
    qi = pl.program_id(1)
    seq = k_ref.shape[0]
    n_units = GQA_TQ // GQA_UNIT
    heads_per_group = GQA_Q_HEADS // GQA_KV_HEADS
    q_refs = (q0_ref, q1_ref)
    s_refs = (s0_ref, s1_ref)
    m_refs = (m0_ref, m1_ref)

    @pl.when(qi == 0)
    def _():
        lane_k = lax.broadcasted_iota(jnp.int32, (seq, LANES), 1)
        kn = _norm_rope(k_ref[...], kg_ref[...], cos_ref[...], sin_ref[...],
                        lane_k < HEAD_DIM, (lane_k % (HEAD_DIM // 2)) < HEAD_DIM // 4)
        kn_ref[...] = kn.astype(BF16)
        vx_ref[:, :KV_WIDTH] = v_ref[...]
        vx_ref[:, KV_WIDTH:] = jnp.ones((seq, LANES), BF16)

    lane = lax.broadcasted_iota(jnp.int32, (GQA_UNIT, LANES), 1)
    lo = lane < HEAD_DIM
    first_half = (lane % (HEAD_DIM // 2)) < HEAD_DIM // 4
    zero = jnp.zeros((GQA_UNIT, LANES), F32)
    scale = HEAD_DIM ** -0.5 * LOG2_E

    def prep_q(u, slot):
        u = jnp.minimum(u, n_units - 1)
        r0 = pl.multiple_of(u * GQA_UNIT, GQA_UNIT)
        t0 = pl.multiple_of(qi * GQA_TQ + u * GQA_UNIT, GQA_UNIT)
        cos = cos_ref[pl.ds(t0, GQA_UNIT), :]
        sin = sin_ref[pl.ds(t0, GQA_UNIT), :]
        stacked = []
        for pair in range(GQA_Q_HEADS // 2):
            qp = q_ref[pl.ds(r0, GQA_UNIT), pair * LANES:(pair + 1) * LANES]
            qn = _norm_rope(qp, qg_ref[...], cos, sin, lo, first_half) * scale
            sw = pltpu.roll(qn, HEAD_DIM, 1)
            if pair < heads_per_group // 2:
                stacked += [jnp.where(lo, qn, zero), jnp.where(lo, sw, zero)]
            else:
                stacked += [jnp.where(lo, zero, sw), jnp.where(lo, zero, qn)]
        q_refs[slot][...] = jnp.concatenate(stacked, axis=0).astype(BF16)

    def scores(slot):
        s = _dot_nt(q_refs[slot][...], kn_ref[...])
        s_refs[slot][...] = s
        m_refs[slot][...] = jnp.broadcast_to(jnp.max(s, axis=-1, keepdims=True),
                                             (GQA_Q_HEADS * GQA_UNIT, LANES))

    def finish(u, slot):
        p = jnp.exp2(s_refs[slot][...] - m_refs[slot][:, :1]).astype(BF16)
        o = _dot(p, vx_ref[...])
        o = o[:, :LANES] / o[:, LANES:]
        r0 = pl.multiple_of(u * GQA_UNIT, GQA_UNIT)
        for pair in range(GQA_Q_HEADS // 2):
            even = o[(2 * pair) * GQA_UNIT:(2 * pair + 1) * GQA_UNIT]
            odd = o[(2 * pair + 1) * GQA_UNIT:(2 * pair + 2) * GQA_UNIT]
            if pair < heads_per_group // 2:
                out = jnp.where(lo, even, pltpu.roll(odd, HEAD_DIM, 1))
            else:
                out = jnp.where(lo, pltpu.roll(even, HEAD_DIM, 1), odd)
            o_ref[pl.ds(r0, GQA_UNIT), pair * LANES:(pair + 1) * LANES] = out.astype(BF16)

    prep_q(0, 0)
    scores(0)
    prep_q(1, 1)

    def body(i, carry):
        scores(1)
        prep_q(2 * i + 2, 0)
        finish(2 * i, 0)
        scores(0)
        prep_q(2 * i + 3, 1)
        finish(2 * i + 1, 1)
        return carry

    lax.fori_loop(0, n_units // 2, body, 0)


def _gqa_attention(q, k, v, cos, sin, qg, kg, batch, seq):
    steps = seq // GQA_TQ
    qspec = pl.BlockSpec((GQA_TQ, GQA_WIDTH), lambda b, i: (b * steps + i, 0))
    kvspec = pl.BlockSpec((seq, KV_WIDTH), lambda b, i: (b, 0))
    stacked_rows = GQA_Q_HEADS * GQA_UNIT
    return pl.pallas_call(
        _gqa_kernel,
        grid=(batch, steps),
        in_specs=[qspec, kvspec, kvspec, _const_spec((seq, LANES)), _const_spec((seq, LANES)),
                  _const_spec((1, LANES)), _const_spec((1, LANES))],
        out_specs=qspec,
        out_shape=jax.ShapeDtypeStruct(q.shape, BF16),
        scratch_shapes=[pltpu.VMEM((seq, KV_WIDTH), BF16), pltpu.VMEM((seq, KV_WIDTH + LANES), BF16),
                        pltpu.VMEM((stacked_rows, LANES), BF16), pltpu.VMEM((stacked_rows, LANES), BF16),
                        pltpu.VMEM((stacked_rows, seq), F32), pltpu.VMEM((stacked_rows, seq), F32),
                        pltpu.VMEM((stacked_rows, LANES), F32), pltpu.VMEM((stacked_rows, LANES), F32)],
        compiler_params=_params("parallel", "arbitrary"),
        name="gqa_attention",
    )(q, k, v, cos, sin, qg, kg)


def _mem_kv_kernel(m_ref, g_ref, w_ref, o_ref):
    h = _rms(m_ref[...], g_ref[...]).astype(BF16)
    o_ref[...] = _dot(h, w_ref[...]).astype(BF16)


def _mem_kv(mem2, g, w):
    n = mem2.shape[0]
    return pl.pallas_call(
        _mem_kv_kernel,
        grid=(n // TM,),
        in_specs=[pl.BlockSpec((TM, D_MODEL), lambda i: (i, 0)), _const_spec((1, D_MODEL)),
                  _const_spec(w.shape)],
        out_specs=pl.BlockSpec((TM, w.shape[1]), lambda i: (i, 0)),
        out_shape=jax.ShapeDtypeStruct((n, w.shape[1]), BF16),
        compiler_params=_params("parallel"),
        name="mem_kv",
    )(mem2, g, w)


def _post_kernel(x_ref, na_ref, ga_ref, wo_ref, g_ref, wq_ref, kv_ref, wmo_ref, o_ref):
    x1 = x_ref[...] + _dot(na_ref[...], wo_ref[:NA_WIDTH, :]) + _dot(ga_ref[...], wo_ref[NA_WIDTH:, :])
    h = _rms(x1, g_ref[...]).astype(BF16)
    q = (_dot(h, wq_ref[...]) * (MEM_HEAD_DIM ** -0.5)).astype(BF16)
    heads = []
    for hh in range(MEM_HEADS):
        cols = slice(hh * MEM_HEAD_DIM, (hh + 1) * MEM_HEAD_DIM)
        s = _dot_nt(q[:, cols], kv_ref[:, cols])
        m = jnp.max(s, axis=-1, keepdims=True)
        p = jnp.exp(s - m)
        l = jnp.sum(p, axis=-1, keepdims=True)
        vcols = slice(MEM_WIDTH + hh * MEM_HEAD_DIM, MEM_WIDTH + (hh + 1) * MEM_HEAD_DIM)
        heads.append((_dot(p.astype(BF16), kv_ref[:, vcols]) / l).astype(BF16))
    o_ref[...] = x1 + _dot(jnp.concatenate(heads, axis=-1), wmo_ref[...])


def _post(x2, na, ga, wo, g, wq, kv, wmo, seq, mem_len):
    n = x2.shape[0]
    tiles_per_seq = seq // TM
    row = lambda width: pl.BlockSpec((TM, width), lambda i: (i, 0))
    return pl.pallas_call(
        _post_kernel,
        grid=(n // TM,),
        in_specs=[row(D_MODEL), row(NA_WIDTH), row(GQA_WIDTH), _const_spec(wo.shape),
                  _const_spec((1, D_MODEL)), _const_spec(wq.shape),
                  pl.BlockSpec((mem_len, 2 * MEM_WIDTH), lambda i: (i // tiles_per_seq, 0)),
                  _const_spec(wmo.shape)],
        out_specs=row(D_MODEL),
        out_shape=jax.ShapeDtypeStruct((n, D_MODEL), F32),
        compiler_params=_params("parallel"),
        name="post_mixer_memattn",
    )(x2, na, ga, wo, g, wq, kv, wmo)


def _ffn_kernel(xp_ref, x_ref, xn_ref, g_ref, wup_ref, cw_ref, cb_ref, wdn_ref, gf_ref, o_ref,
                *, tiles_per_seq, final_norm):
    i = pl.program_id(0)
    first = (i % tiles_per_seq) == 0
    last = (i % tiles_per_seq) == tiles_per_seq - 1
    x = x_ref[...]
    xa = jnp.concatenate([xp_ref[...], x, xn_ref[...]], axis=0)
    ha = _rms(xa, g_ref[...])
    rid = lax.broadcasted_iota(jnp.int32, (TM + 2 * HALO, 1), 0)
    pad = (first & (rid < HALO)) | (last & (rid >= TM + HALO))
    ha = jnp.where(pad, 0.0, ha).astype(BF16)
    n_all = TM + 2 * HALO

    def conv(u, c0):
        cols = slice(c0, c0 + FF_CHUNK)
        prev = pltpu.roll(u, 1, 0)[HALO:HALO + TM]
        nxt = pltpu.roll(u, n_all - 1, 0)[HALO:HALO + TM]
        return (prev * cw_ref[0:1, cols] + u[HALO:HALO + TM] * cw_ref[1:2, cols]
                + nxt * cw_ref[2:3, cols] + cb_ref[:, cols])

    acc = x
    for c in range(D_FF // FF_CHUNK):
        c0 = c * FF_CHUNK
        gate = conv(_dot(ha, wup_ref[:, c0:c0 + FF_CHUNK]), c0)
        val = conv(_dot(ha, wup_ref[:, D_FF + c0:D_FF + c0 + FF_CHUNK]), D_FF + c0)
        act = (gate * jax.nn.sigmoid(gate) * val).astype(BF16)
        acc = acc + _dot(act, wdn_ref[c0:c0 + FF_CHUNK, :])
    if final_norm:
        acc = _rms(acc, gf_ref[...])
    o_ref[...] = acc


def _ffn(x2, g, wup, cw, cb, wdn, gf, seq, final_norm):
    n = x2.shape[0]
    tiles_per_seq = seq // TM
    hb = TM // HALO
    nh = n // HALO
    return pl.pallas_call(
        functools.partial(_ffn_kernel, tiles_per_seq=tiles_per_seq, final_norm=final_norm),
        grid=(n // TM,),
        in_specs=[pl.BlockSpec((HALO, D_MODEL), lambda i: (jnp.maximum(i * hb - 1, 0), 0)),
                  pl.BlockSpec((TM, D_MODEL), lambda i: (i, 0)),
                  pl.BlockSpec((HALO, D_MODEL), lambda i: (jnp.minimum((i + 1) * hb, nh - 1), 0)),
                  _const_spec((1, D_MODEL)), _const_spec(wup.shape), _const_spec(cw.shape),
                  _const_spec(cb.shape), _const_spec(wdn.shape), _const_spec((1, D_MODEL))],
        out_specs=pl.BlockSpec((TM, D_MODEL), lambda i: (i, 0)),
        out_shape=jax.ShapeDtypeStruct((n, D_MODEL), F32),
        compiler_params=_params("parallel"),
        name="conv_ffn",
    )(x2, x2, x2, g, wup, cw, cb, wdn, gf)


def kernel(x, mem, norm_mix, w_in, na_rpb, gqa_q_norm, gqa_k_norm, w_out, norm_mem_q, norm_mem_kv,
           w_mem_q, w_mem_kv, w_mem_o, norm_ffn, w_up, conv_w, conv_b, w_down, norm_final):
    batch, seq, d = x.shape
    depth = w_in.shape[0]
    mem_len = mem.shape[1]
    rows = seq // GRID_W
    assert d == D_MODEL and seq % GRID_W == 0 and rows >= NA_KH and rows % NA_ROWS_PER_STEP == 0
    assert seq % TM == 0 and seq % GQA_TQ == 0 and (batch * mem_len) % TM == 0

    cos, sin = _rope_tables(seq)
    x2 = x.reshape(batch * seq, d)
    mem2 = mem.reshape(batch * mem_len, d)
    row = lambda a: a.reshape(1, -1)
    lanes2 = lambda a: jnp.tile(a, LANES // HEAD_DIM).reshape(1, LANES)

    for l in range(depth):
        naq, nak, nav, gq, gk, gv = _in_proj(x2, row(norm_mix[l]), w_in[l].astype(BF16))
        na = _na_attention(naq, nak, nav, _na_bias_table(na_rpb[l]), batch, seq)
        ga = _gqa_attention(gq, gk, gv, cos, sin, lanes2(gqa_q_norm[l]), lanes2(gqa_k_norm[l]), batch, seq)
        kv = _mem_kv(mem2, row(norm_mem_kv[l]), w_mem_kv[l].astype(BF16))
        x2 = _post(x2, na, ga, w_out[l].astype(BF16), row(norm_mem_q[l]), w_mem_q[l].astype(BF16), kv,
                   w_mem_o[l].astype(BF16), seq, mem_len)
        x2 = _ffn(x2, row(norm_ffn[l]), w_up[l].astype(BF16), conv_w[l], row(conv_b[l]),
                  w_down[l].astype(BF16), row(norm_final), seq, final_norm=(l == depth - 1))
    return x2.reshape(batch, seq, d)
```

```python
import functools

import jax
import jax.numpy as jnp
from jax import lax
from jax.experimental import pallas as pl
from jax.experimental.pallas import tpu as pltpu

D_MODEL = 1024
GRID_W = 64
HEAD_DIM = 64
NA_HEADS = 8
GQA_Q_HEADS = 8
GQA_KV_HEADS = 2
NA_KH = 8
NA_KW = 16
ROPE_THETA = 10000.0
NA_WIDTH = NA_HEADS * HEAD_DIM
GQA_WIDTH = GQA_Q_HEADS * HEAD_DIM
KV_WIDTH = GQA_KV_HEADS * HEAD_DIM
MEM_HEADS = 4
MEM_HEAD_DIM = 128
MEM_WIDTH = MEM_HEADS * MEM_HEAD_DIM
D_FF = 2816
EPS = 1e-6
NEG_INF = -1e30

LANES = 128
VMEM_LIMIT = 56 * 1024 * 1024

TM = 512
HALO = 8
FF_CHUNK = 1408
GQA_TQ = 1024
GQA_UNIT = 64
GQA_ONES_ROWS = 16
GQA_KEY_CHUNK = 512
GQA_KEY_TILE = 256
GQA_CHAINS = 2
NA_ROWS_PER_STEP = 8
LOG2_E = 1.4426950408889634

BF16 = jnp.bfloat16
F32 = jnp.float32


def _rms(x, g):
    return x * lax.rsqrt(jnp.mean(x * x, axis=-1, keepdims=True) + EPS) * g


def _dot(a, b):
    return jnp.dot(a, b, preferred_element_type=F32)


def _dot_nt(a, b):
    return lax.dot_general(a, b, (((1,), (1,)), ((), ())), preferred_element_type=F32)


def _const_spec(shape):
    nd = len(shape)
    return pl.BlockSpec(shape, lambda *_: (0,) * nd, pipeline_mode=pl.Buffered(1))


def _params(*sem):
    return pltpu.CompilerParams(dimension_semantics=sem, vmem_limit_bytes=VMEM_LIMIT)


def _in_proj_kernel(x_ref, g_ref, w_ref, naq_ref, nak_ref, nav_ref, gq_ref, gk_ref, gv_ref):
    h = _rms(x_ref[...], g_ref[...]).astype(BF16)
    o = 0
    naq_ref[...] = (_dot(h, w_ref[:, o:o + NA_WIDTH]) * (HEAD_DIM ** -0.5 * LOG2_E)).astype(BF16)
    o += NA_WIDTH
    nak_ref[...] = _dot(h, w_ref[:, o:o + NA_WIDTH]).astype(BF16)
    o += NA_WIDTH
    nav_ref[...] = _dot(h, w_ref[:, o:o + NA_WIDTH]).astype(BF16)
    o += NA_WIDTH
    gq_ref[...] = _dot(h, w_ref[:, o:o + GQA_WIDTH])
    o += GQA_WIDTH
    gk_ref[...] = _dot(h, w_ref[:, o:o + KV_WIDTH])
    o += KV_WIDTH
    gv_ref[...] = _dot(h, w_ref[:, o:o + KV_WIDTH]).astype(BF16)


def _in_proj(x2, g, w):
    n = x2.shape[0]
    in_width = w.shape[1]
    row = lambda width: pl.BlockSpec((TM, width), lambda i: (i, 0))
    return pl.pallas_call(
        _in_proj_kernel,
        grid=(n // TM,),
        in_specs=[row(D_MODEL), _const_spec((1, D_MODEL)), _const_spec((D_MODEL, in_width))],
        out_specs=[row(NA_WIDTH), row(NA_WIDTH), row(NA_WIDTH), row(GQA_WIDTH), row(KV_WIDTH), row(KV_WIDTH)],
        out_shape=[
            jax.ShapeDtypeStruct((n, NA_WIDTH), BF16),
            jax.ShapeDtypeStruct((n, NA_WIDTH), BF16),
            jax.ShapeDtypeStruct((n, NA_WIDTH), BF16),
            jax.ShapeDtypeStruct((n, GQA_WIDTH), F32),
            jax.ShapeDtypeStruct((n, KV_WIDTH), F32),
            jax.ShapeDtypeStruct((n, KV_WIDTH), BF16),
        ],
        compiler_params=_params("parallel"),
        name="in_proj",
    )(x2, g, w)


N_ROW_OFF = 2 * NA_KH - 1
N_COL_OFF = 2 * NA_KW - 1


def _na_bias_kernel(rpb_ref, o_ref, t_ref):
    cls = pl.program_id(0)
    lane = lax.broadcasted_iota(jnp.int32, (GRID_W, LANES), 1)

    @pl.when(cls == 0)
    def _():
        q = lax.broadcasted_iota(jnp.int32, (GRID_W, LANES), 0)
        k = lane % GRID_W
        col_off = jnp.clip(k - q, -(NA_KW - 1), NA_KW - 1) + (NA_KW - 1)
        col_start = jnp.clip(q - NA_KW // 2, 0, GRID_W - NA_KW)
        valid = (k >= col_start) & (k < col_start + NA_KW)

        def block(b, carry):
            t = jnp.full((GRID_W, LANES), NEG_INF, F32)
            for d in range(N_COL_OFF):
                t = jnp.where(valid & (col_off == d), rpb_ref[b * N_COL_OFF + d], t)
            t_ref[b] = t * LOG2_E
            return carry

        lax.fori_loop(0, NA_HEADS * N_ROW_OFF, block, 0)

    lo = lane < GRID_W
    for h in range(NA_HEADS):
        for ip in range(NA_KH // 2):
            ro = 2 * ip - cls + (NA_KH - 1)
            blk = jnp.where(lo, t_ref[h * N_ROW_OFF + ro], t_ref[h * N_ROW_OFF + ro + 1])
            o_ref[0, h // 2, (h % 2) * GRID_W:(h % 2 + 1) * GRID_W, ip * LANES:(ip + 1) * LANES] = blk


def _na_bias_table(rpb):
    shape = (NA_KH, NA_HEADS // 2, 2 * GRID_W, NA_KH * GRID_W)
    return pl.pallas_call(
        _na_bias_kernel,
        grid=(NA_KH,),
        in_specs=[pl.BlockSpec(memory_space=pltpu.SMEM)],
        out_specs=pl.BlockSpec((1,) + shape[1:], lambda c: (c, 0, 0, 0)),
        out_shape=jax.ShapeDtypeStruct(shape, F32),
        scratch_shapes=[pltpu.VMEM((NA_HEADS * N_ROW_OFF, GRID_W, LANES), F32)],
        compiler_params=_params("arbitrary"),
        name="na_bias_table",
    )(rpb.reshape(-1))


def _na_kernel(q_ref, k_ref, v_ref, bias_ref, o_ref, *, rows):
    rb = pl.program_id(1)
    lo = lax.broadcasted_iota(jnp.int32, (GRID_W, LANES), 1) < HEAD_DIM
    win = NA_KH * GRID_W
    pairs = [slice(j * LANES, (j + 1) * LANES) for j in range(NA_HEADS // 2)]
    ones = jnp.ones((win, LANES), BF16)

    def window(i):
        r = rb * NA_ROWS_PER_STEP + i
        row_start = jnp.clip(r - NA_KH // 2, 0, rows - NA_KH)
        return r - row_start, pl.multiple_of(row_start * GRID_W, GRID_W)

    def scores(i):
        cls, kstart = window(i)
        out = []
        for j, cols in enumerate(pairs):
            qp = q_ref[i * GRID_W:(i + 1) * GRID_W, cols]
            zero = jnp.zeros_like(qp)
            qs = jnp.concatenate([jnp.where(lo, qp, zero), jnp.where(lo, zero, qp)], axis=0)
            out.append(_dot_nt(qs, k_ref[pl.ds(kstart, win), cols]) + bias_ref[cls, j])
        return out

    def finish(i, s_list):
        _, kstart = window(i)
        p_list = [jnp.exp2(s - jnp.max(s, axis=-1, keepdims=True)).astype(BF16) for s in s_list]
        for p, cols in zip(p_list, pairs):
            o = _dot(p, jnp.concatenate([v_ref[pl.ds(kstart, win), cols], ones], axis=1))
            o = o[:, :LANES] / o[:, LANES:]
            o_ref[i * GRID_W:(i + 1) * GRID_W, cols] = jnp.where(lo, o[:GRID_W], o[GRID_W:]).astype(BF16)

    s_next = scores(0)
    for i in range(NA_ROWS_PER_STEP):
        s_cur = s_next
        if i + 1 < NA_ROWS_PER_STEP:
            s_next = scores(i + 1)
        finish(i, s_cur)


def _na_attention(q, k, v, bias, batch, seq):
    rows = seq // GRID_W
    tokens = NA_ROWS_PER_STEP * GRID_W
    steps = rows // NA_ROWS_PER_STEP
    qspec = pl.BlockSpec((tokens, NA_WIDTH), lambda b, r: (b * steps + r, 0))
    kvspec = pl.BlockSpec((seq, NA_WIDTH), lambda b, r: (b, 0))
    return pl.pallas_call(
        functools.partial(_na_kernel, rows=rows),
        grid=(batch, steps),
        in_specs=[qspec, kvspec, kvspec, _const_spec(bias.shape)],
        out_specs=qspec,
        out_shape=jax.ShapeDtypeStruct(q.shape, BF16),
        compiler_params=_params("parallel", "arbitrary"),
        name="na_attention",
    )(q, k, v, bias)


def _rope_tables(seq):
    t = jnp.arange(seq)
    pos = jnp.stack([t // GRID_W, t % GRID_W], axis=-1).astype(F32)
    n_f = HEAD_DIM // 4
    inv_freq = ROPE_THETA ** (-jnp.arange(n_f, dtype=F32) / n_f)
    ang = pos[:, :, None] * inv_freq
    cos, sin = jnp.cos(ang), jnp.sin(ang)
    cos_h = jnp.concatenate([cos, cos], axis=-1).reshape(seq, HEAD_DIM)
    sin_h = jnp.concatenate([-sin, sin], axis=-1).reshape(seq, HEAD_DIM)
    return jnp.tile(cos_h, (1, LANES // HEAD_DIM)), jnp.tile(sin_h, (1, LANES // HEAD_DIM))


def _norm_rope(x, gain, cos, sin, lo, first_half):
    x2 = x * x
    zero = jnp.zeros_like(x2)
    ms_lo = jnp.sum(jnp.where(lo, x2, zero), axis=-1, keepdims=True) * (1.0 / HEAD_DIM)
    ms_hi = jnp.sum(jnp.where(lo, zero, x2), axis=-1, keepdims=True) * (1.0 / HEAD_DIM)
    r = jnp.where(lo, lax.rsqrt(ms_lo + EPS), lax.rsqrt(ms_hi + EPS))
    y = x * r * gain
    n_f = HEAD_DIM // 4
    partner = jnp.where(first_half, pltpu.roll(y, LANES - n_f, 1), pltpu.roll(y, n_f, 1))
    return y * cos + partner * sin


def _gqa_kernel(q_ref, k_ref, v_ref, cos_ref, sin_ref, qg_ref, kg_ref, o_ref, kn_ref, vx_ref, *s_refs):
    qi = pl.program_id(1)
    seq = k_ref.shape[0]
    n_units = GQA_TQ // GQA_UNIT
    n_chunks = seq // GQA_KEY_CHUNK
    heads_per_group = GQA_Q_HEADS // GQA_KV_HEADS
    stacked_rows = GQA_Q_HEADS * GQA_UNIT

    @pl.when(qi == 0)
    def _():
        lane_k = lax.broadcasted_iota(jnp.int32, (seq, LANES), 1)
        kn = _norm_rope(k_ref[...], kg_ref[...], cos_ref[...], sin_ref[...],
                        lane_k < HEAD_DIM, (lane_k % (HEAD_DIM // 2)) < HEAD_DIM // 4)
        kn_ref[...] = kn.astype(BF16)
        vx_ref[:KV_WIDTH, :] = v_ref[...].astype(F32).T.astype(BF16)
        vx_ref[KV_WIDTH:, :] = jnp.ones((GQA_ONES_ROWS, seq), BF16)

    lane = lax.broadcasted_iota(jnp.int32, (GQA_UNIT, LANES), 1)
    lo = lane < HEAD_DIM
    first_half = (lane % (HEAD_DIM // 2)) < HEAD_DIM // 4
    zero = jnp.zeros((GQA_UNIT, LANES), F32)
    scale = HEAD_DIM ** -0.5 * LOG2_E

    def prep_q(u):
        r0 = pl.multiple_of(u * GQA_UNIT, GQA_UNIT)
        t0 = pl.multiple_of(qi * GQA_TQ + u * GQA_UNIT, GQA_UNIT)
        cos = cos_ref[pl.ds(t0, GQA_UNIT), :]
        sin = sin_ref[pl.ds(t0, GQA_UNIT), :]
        stacked = []
        for pair in range(GQA_Q_HEADS // 2):
            qp = q_ref[pl.ds(r0, GQA_UNIT), pair * LANES:(pair + 1) * LANES]
            qn = _norm_rope(qp, qg_ref[...], cos, sin, lo, first_half) * scale
            sw = pltpu.roll(qn, HEAD_DIM, 1)
            if pair < heads_per_group // 2:
                stacked += [jnp.where(lo, qn, zero), jnp.where(lo, sw, zero)]
            else:
                stacked += [jnp.where(lo, zero, sw), jnp.where(lo, zero, qn)]
        return jnp.concatenate(stacked, axis=0).T.astype(BF16)

    def s_scr(k, c):
        return s_refs[2 * k + c % 2]

    def chunk_scores(c, qt, k):
        s = _dot(kn_ref[c * GQA_KEY_CHUNK:(c + 1) * GQA_KEY_CHUNK, :], qt)
        s_scr(k, c)[...] = s
        return jnp.max(s, axis=0, keepdims=True)

    def write_out(u, acc):
        r0 = pl.multiple_of(u * GQA_UNIT, GQA_UNIT)
        o = (acc[:KV_WIDTH] / acc[KV_WIDTH:KV_WIDTH + 1]).T
        for pair in range(GQA_Q_HEADS // 2):
            even = o[(2 * pair) * GQA_UNIT:(2 * pair + 1) * GQA_UNIT]
            odd = o[(2 * pair + 1) * GQA_UNIT:(2 * pair + 2) * GQA_UNIT]
            if pair < heads_per_group // 2:
                out = jnp.where(lo, even, pltpu.roll(odd, HEAD_DIM, 1))
            else:
                out = jnp.where(lo, pltpu.roll(even, HEAD_DIM, 1), odd)
            o_ref[pl.ds(r0, GQA_UNIT), pair * LANES:(pair + 1) * LANES] = out.astype(BF16)

    chains = range(GQA_CHAINS)

    def trip(i, carry):
        qt, cmax, acc_prev = [list(x) for x in carry]
        qt_next = [None] * GQA_CHAINS
        units = [i * GQA_CHAINS + k for k in chains]
        m = [jnp.full((1, stacked_rows), NEG_INF, F32) for _ in chains]
        acc = [jnp.zeros((KV_WIDTH + GQA_ONES_ROWS, stacked_rows), F32) for _ in chains]
        for c in range(n_chunks):
            for k in chains:
                m_new = jnp.maximum(m[k], cmax[k])
                if c + 1 < n_chunks:
                    cmax[k] = chunk_scores(c + 1, qt[k], k)
                else:
                    cmax[k] = chunk_scores(0, qt_next[k], k)
                if c == 0:
                    write_out(jnp.maximum(units[k] - GQA_CHAINS, k), acc_prev[k])
                if c == 1:
                    qt_next[k] = prep_q(jnp.minimum(units[k] + GQA_CHAINS, n_units - GQA_CHAINS + k))
                pv = None
                for t in range(GQA_KEY_CHUNK // GQA_KEY_TILE):
                    rows = slice(t * GQA_KEY_TILE, (t + 1) * GQA_KEY_TILE)
                    k0 = c * GQA_KEY_CHUNK + t * GQA_KEY_TILE
                    p = jnp.exp2(s_scr(k, c)[rows, :] - m_new).astype(BF16)
                    d = _dot(vx_ref[:, k0:k0 + GQA_KEY_TILE], p)
                    pv = d if pv is None else pv + d
                acc[k] = acc[k] * jnp.exp2(m[k] - m_new) + pv
                m[k] = m_new
        return tuple(qt_next), tuple(cmax), tuple(acc)

    assert n_chunks % 2 == 0
    qt0 = tuple(prep_q(k) for k in chains)
    init = (qt0, tuple(chunk_scores(0, qt0[k], k) for k in chains),
            tuple(jnp.ones((KV_WIDTH + GQA_ONES_ROWS, stacked_rows), F32) for _ in chains))
    _, _, acc_last = lax.fori_loop(0, n_units // GQA_CHAINS, trip, init)
    for k in chains:
        write_out(n_units - GQA_CHAINS + k, acc_last[k])


def _gqa_attention(q, k, v, cos, sin, qg, kg, batch, seq):
    steps = seq // GQA_TQ
    qspec = pl.BlockSpec((GQA_TQ, GQA_WIDTH), lambda b, i: (b * steps + i, 0))
    kvspec = pl.BlockSpec((seq, KV_WIDTH), lambda b, i: (b, 0))
    return pl.pallas_call(
        _gqa_kernel,
        grid=(batch, steps),
        in_specs=[qspec, kvspec, kvspec, _const_spec((seq, LANES)), _const_spec((seq, LANES)),
                  _const_spec((1, LANES)), _const_spec((1, LANES))],
        out_specs=qspec,
        out_shape=jax.ShapeDtypeStruct(q.shape, BF16),
        scratch_shapes=[pltpu.VMEM((seq, KV_WIDTH), BF16), pltpu.VMEM((KV_WIDTH + GQA_ONES_ROWS, seq), BF16)]
        + [pltpu.VMEM((GQA_KEY_CHUNK, GQA_Q_HEADS * GQA_UNIT), F32)] * (2 * GQA_CHAINS),
        compiler_params=_params("parallel", "arbitrary"),
        name="gqa_attention",
    )(q, k, v, cos, sin, qg, kg)


def _mem_kv_kernel(m_ref, g_ref, w_ref, o_ref):
    h = _rms(m_ref[...], g_ref[...]).astype(BF16)
    o_ref[...] = _dot(h, w_ref[...]).astype(BF16)


def _mem_kv(mem2, g, w):
    n = mem2.shape[0]
    return pl.pallas_call(
        _mem_kv_kernel,
        grid=(n // TM,),
        in_specs=[pl.BlockSpec((TM, D_MODEL), lambda i: (i, 0)), _const_spec((1, D_MODEL)),
                  _const_spec(w.shape)],
        out_specs=pl.BlockSpec((TM, w.shape[1]), lambda i: (i, 0)),
        out_shape=jax.ShapeDtypeStruct((n, w.shape[1]), BF16),
        compiler_params=_params("parallel"),
        name="mem_kv",
    )(mem2, g, w)


def _post_kernel(x_ref, na_ref, ga_ref, wo_ref, g_ref, wq_ref, kv_ref, wmo_ref, o_ref):
    x1 = x_ref[...] + _dot(na_ref[...], wo_ref[:NA_WIDTH, :]) + _dot(ga_ref[...], wo_ref[NA_WIDTH:, :])
    h = _rms(x1, g_ref[...]).astype(BF16)
    q = (_dot(h, wq_ref[...]) * (MEM_HEAD_DIM ** -0.5)).astype(BF16)
    heads = []
    for hh in range(MEM_HEADS):
        cols = slice(hh * MEM_HEAD_DIM, (hh + 1) * MEM_HEAD_DIM)
        s = _dot_nt(q[:, cols], kv_ref[:, cols])
        m = jnp.max(s, axis=-1, keepdims=True)
        p = jnp.exp(s - m)
        l = jnp.sum(p, axis=-1, keepdims=True)
        vcols = slice(MEM_WIDTH + hh * MEM_HEAD_DIM, MEM_WIDTH + (hh + 1) * MEM_HEAD_DIM)
        heads.append((_dot(p.astype(BF16), kv_ref[:, vcols]) / l).astype(BF16))
    o_ref[...] = x1 + _dot(jnp.concatenate(heads, axis=-1), wmo_ref[...])


def _post(x2, na, ga, wo, g, wq, kv, wmo, seq, mem_len):
    n = x2.shape[0]
    tiles_per_seq = seq // TM
    row = lambda width: pl.BlockSpec((TM, width), lambda i: (i, 0))
    return pl.pallas_call(
        _post_kernel,
        grid=(n // TM,),
        in_specs=[row(D_MODEL), row(NA_WIDTH), row(GQA_WIDTH), _const_spec(wo.shape),
                  _const_spec((1, D_MODEL)), _const_spec(wq.shape),
                  pl.BlockSpec((mem_len, 2 * MEM_WIDTH), lambda i: (i // tiles_per_seq, 0)),
                  _const_spec(wmo.shape)],
        out_specs=row(D_MODEL),
        out_shape=jax.ShapeDtypeStruct((n, D_MODEL), F32),
        compiler_params=_params("parallel"),
        name="post_mixer_memattn",
    )(x2, na, ga, wo, g, wq, kv, wmo)


def _ffn_kernel(xp_ref, x_ref, xn_ref, g_ref, wup_ref, cw_ref, cb_ref, wdn_ref, gf_ref, o_ref,
                *, tiles_per_seq, final_norm):
    i = pl.program_id(0)
    first = (i % tiles_per_seq) == 0
    last = (i % tiles_per_seq) == tiles_per_seq - 1
    x = x_ref[...]
    xa = jnp.concatenate([xp_ref[...], x, xn_ref[...]], axis=0)
    ha = _rms(xa, g_ref[...])
    rid = lax.broadcasted_iota(jnp.int32, (TM + 2 * HALO, 1), 0)
    pad = (first & (rid < HALO)) | (last & (rid >= TM + HALO))
    ha = jnp.where(pad, 0.0, ha).astype(BF16)
    n_all = TM + 2 * HALO

    def conv(u, c0):
        cols = slice(c0, c0 + FF_CHUNK)
        prev = pltpu.roll(u, 1, 0)[HALO:HALO + TM]
        nxt = pltpu.roll(u, n_all - 1, 0)[HALO:HALO + TM]
        return (prev * cw_ref[0:1, cols] + u[HALO:HALO + TM] * cw_ref[1:2, cols]
                + nxt * cw_ref[2:3, cols] + cb_ref[:, cols])

    acc = x
    for c in range(D_FF // FF_CHUNK):
        c0 = c * FF_CHUNK
        gate = conv(_dot(ha, wup_ref[:, c0:c0 + FF_CHUNK]), c0)
        val = conv(_dot(ha, wup_ref[:, D_FF + c0:D_FF + c0 + FF_CHUNK]), D_FF + c0)
        act = (gate * jax.nn.sigmoid(gate) * val).astype(BF16)
        acc = acc + _dot(act, wdn_ref[c0:c0 + FF_CHUNK, :])
    if final_norm:
        acc = _rms(acc, gf_ref[...])
    o_ref[...] = acc


def _ffn(x2, g, wup, cw, cb, wdn, gf, seq, final_norm):
    n = x2.shape[0]
    tiles_per_seq = seq // TM
    hb = TM // HALO
    nh = n // HALO
    return pl.pallas_call(
        functools.partial(_ffn_kernel, tiles_per_seq=tiles_per_seq, final_norm=final_norm),
        grid=(n // TM,),
        in_specs=[pl.BlockSpec((HALO, D_MODEL), lambda i: (jnp.maximum(i * hb - 1, 0), 0)),
                  pl.BlockSpec((TM, D_MODEL), lambda i: (i, 0)),
                  pl.BlockSpec((HALO, D_MODEL), lambda i: (jnp.minimum((i + 1) * hb, nh - 1), 0)),
                  _const_spec((1, D_MODEL)), _const_spec(wup.shape), _const_spec(cw.shape),
                  _const_spec(cb.shape), _const_spec(wdn.shape), _const_spec((1, D_MODEL))],
        out_specs=pl.BlockSpec((TM, D_MODEL), lambda i: (i, 0)),
        out_shape=jax.ShapeDtypeStruct((n, D_MODEL), F32),
        compiler_params=_params("parallel"),
        name="conv_ffn",
    )(x2, x2, x2, g, wup, cw, cb, wdn, gf)


def kernel(x, mem, norm_mix, w_in, na_rpb, gqa_q_norm, gqa_k_norm, w_out, norm_mem_q, norm_mem_kv,
           w_mem_q, w_mem_kv, w_mem_o, norm_ffn, w_up, conv_w, conv_b, w_down, norm_final):
    batch, seq, d = x.shape
    depth = w_in.shape[0]
    mem_len = mem.shape[1]
    rows = seq // GRID_W
    assert d == D_MODEL and seq % GRID_W == 0 and rows >= NA_KH and rows % NA_ROWS_PER_STEP == 0
    assert seq % TM == 0 and seq % GQA_TQ == 0 and (batch * mem_len) % TM == 0

    cos, sin = _rope_tables(seq)
    x2 = x.reshape(batch * seq, d)
    mem2 = mem.reshape(batch * mem_len, d)
    row = lambda a: a.reshape(1, -1)
    lanes2 = lambda a: jnp.tile(a, LANES // HEAD_DIM).reshape(1, LANES)

    for l in range(depth):
        naq, nak, nav, gq, gk, gv = _in_proj(x2, row(norm_mix[l]), w_in[l].astype(BF16))
        na = _na_attention(naq, nak, nav, _na_bias_table(na_rpb[l]), batch, seq)
        ga = _gqa_attention(gq, gk, gv, cos, sin, lanes2(gqa_q_norm[l]), lanes2(gqa_k_norm[l]), batch, seq)
        kv = _mem_kv(mem2, row(norm_mem_kv[l]), w_mem_kv[l].astype(BF16))
        x2 = _post(x2, na, ga, w_out[l].astype(BF16), row(norm_mem_q[l]), w_mem_q[l].astype(BF16), kv,
                   w_mem_o[l].astype(BF16), seq, mem_len)
        x2 = _ffn(x2, row(norm_ffn[l]), w_up[l].astype(BF16), conv_w[l], row(conv_b[l]),
                  w_down[l].astype(BF16), row(norm_final), seq, final_norm=(l == depth - 1))
    return x2.reshape(batch, seq, d)
```

```python
import functools

import jax
import jax.numpy as jnp
from jax import lax
from jax.experimental import pallas as pl
from jax.experimental.pallas import tpu as pltpu

D_MODEL = 1024
GRID_W = 64
HEAD_DIM = 64
NA_HEADS = 8
GQA_Q_HEADS = 8
GQA_KV_HEADS = 2
NA_KH = 8
NA_KW = 16
ROPE_THETA = 10000.0
NA_WIDTH = NA_HEADS * HEAD_DIM
GQA_WIDTH = GQA_Q_HEADS * HEAD_DIM
KV_WIDTH = GQA_KV_HEADS * HEAD_DIM
MEM_HEADS = 4
MEM_HEAD_DIM = 128
MEM_WIDTH = MEM_HEADS * MEM_HEAD_DIM
D_FF = 2816
EPS = 1e-6
NEG_INF = -1e30

LANES = 128
VMEM_LIMIT = 56 * 1024 * 1024

TM = 512
TM_PROJ = 1024
HALO = 8
FF_CHUNK = 1408
GQA_TQ = 1024
GQA_UNIT = 64
GQA_ONES_ROWS = 16
GQA_KEY_CHUNK = 512
GQA_KEY_TILE = 256
GQA_CHAINS = 2
GQA_SAFE_LOGIT = 100.0
GQA_BOUND_SLACK = 1.02
NA_ROWS_PER_STEP = 8
LOG2_E = 1.4426950408889634

BF16 = jnp.bfloat16
F32 = jnp.float32


def _rms(x, g):
    return x * lax.rsqrt(jnp.mean(x * x, axis=-1, keepdims=True) + EPS) * g


def _dot(a, b):
    return jnp.dot(a, b, preferred_element_type=F32)


def _dot_nt(a, b):
    return lax.dot_general(a, b, (((1,), (1,)), ((), ())), preferred_element_type=F32)


def _const_spec(shape):
    nd = len(shape)
    return pl.BlockSpec(shape, lambda *_: (0,) * nd, pipeline_mode=pl.Buffered(1))


def _params(*sem):
    return pltpu.CompilerParams(dimension_semantics=sem, vmem_limit_bytes=VMEM_LIMIT)


def _in_proj_kernel(x_ref, g_ref, w_ref, naq_ref, nak_ref, nav_ref, gq_ref, gk_ref, gv_ref):
    h = _rms(x_ref[...], g_ref[...]).astype(BF16)
    o = 0
    naq_ref[...] = (_dot(h, w_ref[:, o:o + NA_WIDTH]) * (HEAD_DIM ** -0.5 * LOG2_E)).astype(BF16)
    o += NA_WIDTH
    nak_ref[...] = _dot(h, w_ref[:, o:o + NA_WIDTH]).astype(BF16)
    o += NA_WIDTH
    nav_ref[...] = _dot(h, w_ref[:, o:o + NA_WIDTH]).astype(BF16)
    o += NA_WIDTH
    gq_ref[...] = _dot(h, w_ref[:, o:o + GQA_WIDTH])
    o += GQA_WIDTH
    gk_ref[...] = _dot(h, w_ref[:, o:o + KV_WIDTH])
    o += KV_WIDTH
    gv_ref[...] = _dot(h, w_ref[:, o:o + KV_WIDTH]).astype(BF16)


def _in_proj(x2, g, w):
    n = x2.shape[0]
    in_width = w.shape[1]
    row = lambda width: pl.BlockSpec((TM_PROJ, width), lambda i: (i, 0))
    return pl.pallas_call(
        _in_proj_kernel,
        grid=(n // TM_PROJ,),
        in_specs=[row(D_MODEL), _const_spec((1, D_MODEL)), _const_spec((D_MODEL, in_width))],
        out_specs=[row(NA_WIDTH), row(NA_WIDTH), row(NA_WIDTH), row(GQA_WIDTH), row(KV_WIDTH), row(KV_WIDTH)],
        out_shape=[
            jax.ShapeDtypeStruct((n, NA_WIDTH), BF16),
            jax.ShapeDtypeStruct((n, NA_WIDTH), BF16),
            jax.ShapeDtypeStruct((n, NA_WIDTH), BF16),
            jax.ShapeDtypeStruct((n, GQA_WIDTH), F32),
            jax.ShapeDtypeStruct((n, KV_WIDTH), F32),
            jax.ShapeDtypeStruct((n, KV_WIDTH), BF16),
        ],
        compiler_params=_params("parallel"),
        name="in_proj",
    )(x2, g, w)


N_ROW_OFF = 2 * NA_KH - 1
N_COL_OFF = 2 * NA_KW - 1


def _na_bias_kernel(rpb_ref, o_ref, t_ref):
    cls = pl.program_id(0)
    lane = lax.broadcasted_iota(jnp.int32, (GRID_W, LANES), 1)

    @pl.when(cls == 0)
    def _():
        q = lax.broadcasted_iota(jnp.int32, (GRID_W, LANES), 0)
        k = lane % GRID_W
        col_off = jnp.clip(k - q, -(NA_KW - 1), NA_KW - 1) + (NA_KW - 1)
        col_start = jnp.clip(q - NA_KW // 2, 0, GRID_W - NA_KW)
        valid = (k >= col_start) & (k < col_start + NA_KW)

        def block(b, carry):
            t = jnp.full((GRID_W, LANES), NEG_INF, F32)
            for d in range(N_COL_OFF):
                t = jnp.where(valid & (col_off == d), rpb_ref[b * N_COL_OFF + d], t)
            t_ref[b] = t * LOG2_E
            return carry

        lax.fori_loop(0, NA_HEADS * N_ROW_OFF, block, 0)

    lo = lane < GRID_W
    for h in range(NA_HEADS):
        for ip in range(NA_KH // 2):
            ro = 2 * ip - cls + (NA_KH - 1)
            blk = jnp.where(lo, t_ref[h * N_ROW_OFF + ro], t_ref[h * N_ROW_OFF + ro + 1])
            o_ref[0, h // 2, (h % 2) * GRID_W:(h % 2 + 1) * GRID_W, ip * LANES:(ip + 1) * LANES] = blk


def _na_bias_table(rpb):
    shape = (NA_KH, NA_HEADS // 2, 2 * GRID_W, NA_KH * GRID_W)
    return pl.pallas_call(
        _na_bias_kernel,
        grid=(NA_KH,),
        in_specs=[pl.BlockSpec(memory_space=pltpu.SMEM)],
        out_specs=pl.BlockSpec((1,) + shape[1:], lambda c: (c, 0, 0, 0)),
        out_shape=jax.ShapeDtypeStruct(shape, F32),
        scratch_shapes=[pltpu.VMEM((NA_HEADS * N_ROW_OFF, GRID_W, LANES), F32)],
        compiler_params=_params("arbitrary"),
        name="na_bias_table",
    )(rpb.reshape(-1))


def _na_kernel(q_ref, k_ref, v_ref, bias_ref, o_ref, *, rows):
    rb = pl.program_id(1)
    lo = lax.broadcasted_iota(jnp.int32, (GRID_W, LANES), 1) < HEAD_DIM
    win = NA_KH * GRID_W
    pairs = [slice(j * LANES, (j + 1) * LANES) for j in range(NA_HEADS // 2)]
    ones = jnp.ones((win, LANES), BF16)

    def window(i):
        r = rb * NA_ROWS_PER_STEP + i
        row_start = jnp.clip(r - NA_KH // 2, 0, rows - NA_KH)
        return r - row_start, pl.multiple_of(row_start * GRID_W, GRID_W)

    def scores(i):
        cls, kstart = window(i)
        out = []
        for j, cols in enumerate(pairs):
            qp = q_ref[i * GRID_W:(i + 1) * GRID_W, cols]
            zero = jnp.zeros_like(qp)
            qs = jnp.concatenate([jnp.where(lo, qp, zero), jnp.where(lo, zero, qp)], axis=0)
            out.append(_dot_nt(qs, k_ref[pl.ds(kstart, win), cols]) + bias_ref[cls, j])
        return out

    def finish(i, s_list):
        _, kstart = window(i)
        p_list = [jnp.exp2(s - jnp.max(s, axis=-1, keepdims=True)).astype(BF16) for s in s_list]
        for p, cols in zip(p_list, pairs):
            o = _dot(p, jnp.concatenate([v_ref[pl.ds(kstart, win), cols], ones], axis=1))
            o = o[:, :LANES] / o[:, LANES:]
            o_ref[i * GRID_W:(i + 1) * GRID_W, cols] = jnp.where(lo, o[:GRID_W], o[GRID_W:]).astype(BF16)

    s_next = scores(0)
    for i in range(NA_ROWS_PER_STEP):
        s_cur = s_next
        if i + 1 < NA_ROWS_PER_STEP:
            s_next = scores(i + 1)
        finish(i, s_cur)


def _na_attention(q, k, v, bias, batch, seq):
    rows = seq // GRID_W
    tokens = NA_ROWS_PER_STEP * GRID_W
    steps = rows // NA_ROWS_PER_STEP
    qspec = pl.BlockSpec((tokens, NA_WIDTH), lambda b, r: (b * steps + r, 0))
    kvspec = pl.BlockSpec((seq, NA_WIDTH), lambda b, r: (b, 0))
    return pl.pallas_call(
        functools.partial(_na_kernel, rows=rows),
        grid=(batch, steps),
        in_specs=[qspec, kvspec, kvspec, _const_spec(bias.shape)],
        out_specs=qspec,
        out_shape=jax.ShapeDtypeStruct(q.shape, BF16),
        compiler_params=_params("parallel", "arbitrary"),
        name="na_attention",
    )(q, k, v, bias)


def _rope_tables(seq):
    t = jnp.arange(seq)
    pos = jnp.stack([t // GRID_W, t % GRID_W], axis=-1).astype(F32)
    n_f = HEAD_DIM // 4
    inv_freq = ROPE_THETA ** (-jnp.arange(n_f, dtype=F32) / n_f)
    ang = pos[:, :, None] * inv_freq
    cos, sin = jnp.cos(ang), jnp.sin(ang)
    cos_h = jnp.concatenate([cos, cos], axis=-1).reshape(seq, HEAD_DIM)
    sin_h = jnp.concatenate([-sin, sin], axis=-1).reshape(seq, HEAD_DIM)
    return jnp.tile(cos_h, (1, LANES // HEAD_DIM)), jnp.tile(sin_h, (1, LANES // HEAD_DIM))


def _norm_rope(x, gain, cos, sin, lo, first_half):
    x2 = x * x
    zero = jnp.zeros_like(x2)
    ms_lo = jnp.sum(jnp.where(lo, x2, zero), axis=-1, keepdims=True) * (1.0 / HEAD_DIM)
    ms_hi = jnp.sum(jnp.where(lo, zero, x2), axis=-1, keepdims=True) * (1.0 / HEAD_DIM)
    r = jnp.where(lo, lax.rsqrt(ms_lo + EPS), lax.rsqrt(ms_hi + EPS))
    y = x * r * gain
    n_f = HEAD_DIM // 4
    partner = jnp.where(first_half, pltpu.roll(y, LANES - n_f, 1), pltpu.roll(y, n_f, 1))
    return y * cos + partner * sin


def _gqa_kernel(q_ref, k_ref, v_ref, cos_ref, sin_ref, qg_ref, kg_ref, o_ref, kn_ref, vx_ref, *s_refs):
    qi = pl.program_id(1)
    seq = k_ref.shape[0]
    n_units = GQA_TQ // GQA_UNIT
    n_chunks = seq // GQA_KEY_CHUNK
    heads_per_group = GQA_Q_HEADS // GQA_KV_HEADS
    stacked_rows = GQA_Q_HEADS * GQA_UNIT

    @pl.when(qi == 0)
    def _():
        lane_k = lax.broadcasted_iota(jnp.int32, (seq, LANES), 1)
        kn = _norm_rope(k_ref[...], kg_ref[...], cos_ref[...], sin_ref[...],
                        lane_k < HEAD_DIM, (lane_k % (HEAD_DIM // 2)) < HEAD_DIM // 4)
        kn_ref[...] = kn.astype(BF16)
        vx_ref[:KV_WIDTH, :] = v_ref[...].astype(F32).T.astype(BF16)
        vx_ref[KV_WIDTH:, :] = jnp.ones((GQA_ONES_ROWS, seq), BF16)

    lane = lax.broadcasted_iota(jnp.int32, (GQA_UNIT, LANES), 1)
    lo = lane < HEAD_DIM
    first_half = (lane % (HEAD_DIM // 2)) < HEAD_DIM // 4
    zero = jnp.zeros((GQA_UNIT, LANES), F32)
    scale = HEAD_DIM ** -0.5 * LOG2_E

    def prep_q(u):
        r0 = pl.multiple_of(u * GQA_UNIT, GQA_UNIT)
        t0 = pl.multiple_of(qi * GQA_TQ + u * GQA_UNIT, GQA_UNIT)
        cos = cos_ref[pl.ds(t0, GQA_UNIT), :]
        sin = sin_ref[pl.ds(t0, GQA_UNIT), :]
        stacked = []
        for pair in range(GQA_Q_HEADS // 2):
            qp = q_ref[pl.ds(r0, GQA_UNIT), pair * LANES:(pair + 1) * LANES]
            qn = _norm_rope(qp, qg_ref[...], cos, sin, lo, first_half) * scale
            sw = pltpu.roll(qn, HEAD_DIM, 1)
            if pair < heads_per_group // 2:
                stacked += [jnp.where(lo, qn, zero), jnp.where(lo, sw, zero)]
            else:
                stacked += [jnp.where(lo, zero, sw), jnp.where(lo, zero, qn)]
        return jnp.concatenate(stacked, axis=0).T.astype(BF16)

    def s_scr(k, c):
        return s_refs[2 * k + c % 2]

    def chunk_scores(c, qt, k, with_max=True):
        s = _dot(kn_ref[c * GQA_KEY_CHUNK:(c + 1) * GQA_KEY_CHUNK, :], qt)
        s_scr(k, c)[...] = s
        return jnp.max(s, axis=0, keepdims=True) if with_max else None

    def write_out(u, acc):
        r0 = pl.multiple_of(u * GQA_UNIT, GQA_UNIT)
        o = (acc[:KV_WIDTH] / acc[KV_WIDTH:KV_WIDTH + 1]).T
        for pair in range(GQA_Q_HEADS // 2):
            even = o[(2 * pair) * GQA_UNIT:(2 * pair + 1) * GQA_UNIT]
            odd = o[(2 * pair + 1) * GQA_UNIT:(2 * pair + 2) * GQA_UNIT]
            if pair < heads_per_group // 2:
                out = jnp.where(lo, even, pltpu.roll(odd, HEAD_DIM, 1))
            else:
                out = jnp.where(lo, pltpu.roll(even, HEAD_DIM, 1), odd)
            o_ref[pl.ds(r0, GQA_UNIT), pair * LANES:(pair + 1) * LANES] = out.astype(BF16)

    chains = range(GQA_CHAINS)

    def trip(i, carry):
        qt, cmax, acc_prev = [list(x) for x in carry]
        qt_next = [None] * GQA_CHAINS
        units = [i * GQA_CHAINS + k for k in chains]
        m = [jnp.full((1, stacked_rows), NEG_INF, F32) for _ in chains]
        acc = [jnp.zeros((KV_WIDTH + GQA_ONES_ROWS, stacked_rows), F32) for _ in chains]
        for c in range(n_chunks):
            for k in chains:
                m_new = jnp.maximum(m[k], cmax[k])
                if c + 1 < n_chunks:
                    cmax[k] = chunk_scores(c + 1, qt[k], k)
                else:
                    cmax[k] = chunk_scores(0, qt_next[k], k)
                if c == 0:
                    write_out(jnp.maximum(units[k] - GQA_CHAINS, k), acc_prev[k])
                if c == 1:
                    qt_next[k] = prep_q(jnp.minimum(units[k] + GQA_CHAINS, n_units - GQA_CHAINS + k))
                pv = None
                for t in range(GQA_KEY_CHUNK // GQA_KEY_TILE):
                    rows = slice(t * GQA_KEY_TILE, (t + 1) * GQA_KEY_TILE)
                    k0 = c * GQA_KEY_CHUNK + t * GQA_KEY_TILE
                    p = jnp.exp2(s_scr(k, c)[rows, :] - m_new).astype(BF16)
                    d = _dot(vx_ref[:, k0:k0 + GQA_KEY_TILE], p)
                    pv = d if pv is None else pv + d
                acc[k] = acc[k] * jnp.exp2(m[k] - m_new) + pv
                m[k] = m_new
        return tuple(qt_next), tuple(cmax), tuple(acc)

    def trip_bounded(i, carry):
        qt, acc_prev = [list(x) for x in carry]
        qt_next = [None] * GQA_CHAINS
        units = [i * GQA_CHAINS + k for k in chains]
        acc = [None] * GQA_CHAINS
        for c in range(n_chunks):
            rows = slice(c * GQA_KEY_CHUNK, (c + 1) * GQA_KEY_CHUNK)
            for k in chains:
                if c + 1 < n_chunks:
                    chunk_scores(c + 1, qt[k], k, with_max=False)
                else:
                    chunk_scores(0, qt_next[k], k, with_max=False)
                if c == 0:
                    write_out(jnp.maximum(units[k] - GQA_CHAINS, k), acc_prev[k])
                if c == 1:
                    qt_next[k] = prep_q(jnp.minimum(units[k] + GQA_CHAINS, n_units - GQA_CHAINS + k))
                d = _dot(vx_ref[:, rows], jnp.exp2(s_scr(k, c)[...]).astype(BF16))
                acc[k] = d if acc[k] is None else acc[k] + d
        return tuple(qt_next), tuple(acc)

    placeholder = tuple(jnp.ones((KV_WIDTH + GQA_ONES_ROWS, stacked_rows), F32) for _ in chains)

    def run_exact():
        assert n_chunks % 2 == 0
        qt0 = tuple(prep_q(k) for k in chains)
        init = (qt0, tuple(chunk_scores(0, qt0[k], k) for k in chains), placeholder)
        _, _, acc_last = lax.fori_loop(0, n_units // GQA_CHAINS, trip, init)
        for k in chains:
            write_out(n_units - GQA_CHAINS + k, acc_last[k])

    def run_bounded():
        qt0 = tuple(prep_q(k) for k in chains)
        for k in chains:
            chunk_scores(0, qt0[k], k, with_max=False)
        init = (qt0, placeholder)
        _, acc_last = lax.fori_loop(0, n_units // GQA_CHAINS, trip_bounded, init)
        for k in chains:
            write_out(n_units - GQA_CHAINS + k, acc_last[k])

    logit_bound = (HEAD_DIM * scale * GQA_BOUND_SLACK) * jnp.max(jnp.abs(qg_ref[...])) * jnp.max(jnp.abs(kg_ref[...]))
    lax.cond(logit_bound <= GQA_SAFE_LOGIT, run_bounded, run_exact)


def _gqa_attention(q, k, v, cos, sin, qg, kg, batch, seq):
    steps = seq // GQA_TQ
    qspec = pl.BlockSpec((GQA_TQ, GQA_WIDTH), lambda b, i: (b * steps + i, 0))
    kvspec = pl.BlockSpec((seq, KV_WIDTH), lambda b, i: (b, 0))
    return pl.pallas_call(
        _gqa_kernel,
        grid=(batch, steps),
        in_specs=[qspec, kvspec, kvspec, _const_spec((seq, LANES)), _const_spec((seq, LANES)),
                  _const_spec((1, LANES)), _const_spec((1, LANES))],
        out_specs=qspec,
        out_shape=jax.ShapeDtypeStruct(q.shape, BF16),
        scratch_shapes=[pltpu.VMEM((seq, KV_WIDTH), BF16), pltpu.VMEM((KV_WIDTH + GQA_ONES_ROWS, seq), BF16)]
        + [pltpu.VMEM((GQA_KEY_CHUNK, GQA_Q_HEADS * GQA_UNIT), F32)] * (2 * GQA_CHAINS),
        compiler_params=_params("parallel", "arbitrary"),
        name="gqa_attention",
    )(q, k, v, cos, sin, qg, kg)


def _mem_kv_kernel(m_ref, g_ref, w_ref, o_ref):
    h = _rms(m_ref[...], g_ref[...]).astype(BF16)
    o_ref[...] = _dot(h, w_ref[...]).astype(BF16)


def _mem_kv(mem2, g, w):
    n = mem2.shape[0]
    return pl.pallas_call(
        _mem_kv_kernel,
        grid=(n // TM,),
        in_specs=[pl.BlockSpec((TM, D_MODEL), lambda i: (i, 0)), _const_spec((1, D_MODEL)),
                  _const_spec(w.shape)],
        out_specs=pl.BlockSpec((TM, w.shape[1]), lambda i: (i, 0)),
        out_shape=jax.ShapeDtypeStruct((n, w.shape[1]), BF16),
        compiler_params=_params("parallel"),
        name="mem_kv",
    )(mem2, g, w)


def _post_kernel(x_ref, na_ref, ga_ref, wo_ref, g_ref, wq_ref, kv_ref, wmo_ref, o_ref):
    x1 = x_ref[...] + _dot(na_ref[...], wo_ref[:NA_WIDTH, :]) + _dot(ga_ref[...], wo_ref[NA_WIDTH:, :])
    h = _rms(x1, g_ref[...]).astype(BF16)
    q = (_dot(h, wq_ref[...]) * (MEM_HEAD_DIM ** -0.5)).astype(BF16)
    heads = []
    for hh in range(MEM_HEADS):
        cols = slice(hh * MEM_HEAD_DIM, (hh + 1) * MEM_HEAD_DIM)
        s = _dot_nt(q[:, cols], kv_ref[:, cols])
        m = jnp.max(s, axis=-1, keepdims=True)
        p = jnp.exp(s - m)
        l = jnp.sum(p, axis=-1, keepdims=True)
        vcols = slice(MEM_WIDTH + hh * MEM_HEAD_DIM, MEM_WIDTH + (hh + 1) * MEM_HEAD_DIM)
        heads.append((_dot(p.astype(BF16), kv_ref[:, vcols]) / l).astype(BF16))
    o_ref[...] = x1 + _dot(jnp.concatenate(heads, axis=-1), wmo_ref[...])


def _post(x2, na, ga, wo, g, wq, kv, wmo, seq, mem_len):
    n = x2.shape[0]
    tiles_per_seq = seq // TM_PROJ
    row = lambda width: pl.BlockSpec((TM_PROJ, width), lambda i: (i, 0))
    return pl.pallas_call(
        _post_kernel,
        grid=(n // TM_PROJ,),
        in_specs=[row(D_MODEL), row(NA_WIDTH), row(GQA_WIDTH), _const_spec(wo.shape),
                  _const_spec((1, D_MODEL)), _const_spec(wq.shape),
                  pl.BlockSpec((mem_len, 2 * MEM_WIDTH), lambda i: (i // tiles_per_seq, 0)),
                  _const_spec(wmo.shape)],
        out_specs=row(D_MODEL),
        out_shape=jax.ShapeDtypeStruct((n, D_MODEL), F32),
        compiler_params=_params("parallel"),
        name="post_mixer_memattn",
    )(x2, na, ga, wo, g, wq, kv, wmo)


def _ffn_kernel(xp_ref, x_ref, xn_ref, g_ref, wup_ref, cw_ref, cb_ref, wdn_ref, gf_ref, o_ref,
                *, tiles_per_seq, final_norm):
    i = pl.program_id(0)
    first = (i % tiles_per_seq) == 0
    last = (i % tiles_per_seq) == tiles_per_seq - 1
    x = x_ref[...]
    xa = jnp.concatenate([xp_ref[...], x, xn_ref[...]], axis=0)
    ha = _rms(xa, g_ref[...])
    rid = lax.broadcasted_iota(jnp.int32, (TM + 2 * HALO, 1), 0)
    pad = (first & (rid < HALO)) | (last & (rid >= TM + HALO))
    ha = jnp.where(pad, 0.0, ha).astype(BF16)
    n_all = TM + 2 * HALO

    def conv(u, c0):
        cols = slice(c0, c0 + FF_CHUNK)
        prev = pltpu.roll(u, 1, 0)[HALO:HALO + TM]
        nxt = pltpu.roll(u, n_all - 1, 0)[HALO:HALO + TM]
        return (prev * cw_ref[0:1, cols] + u[HALO:HALO + TM] * cw_ref[1:2, cols]
                + nxt * cw_ref[2:3, cols] + cb_ref[:, cols])

    acc = x
    for c in range(D_FF // FF_CHUNK):
        c0 = c * FF_CHUNK
        gate = conv(_dot(ha, wup_ref[:, c0:c0 + FF_CHUNK]), c0)
        val = conv(_dot(ha, wup_ref[:, D_FF + c0:D_FF + c0 + FF_CHUNK]), D_FF + c0)
        act = (gate * jax.nn.sigmoid(gate) * val).astype(BF16)
        acc = acc + _dot(act, wdn_ref[c0:c0 + FF_CHUNK, :])
    if final_norm:
        acc = _rms(acc, gf_ref[...])
    o_ref[...] = acc


def _ffn(x2, g, wup, cw, cb, wdn, gf, seq, final_norm):
    n = x2.shape[0]
    tiles_per_seq = seq // TM
    hb = TM // HALO
    nh = n // HALO
    return pl.pallas_call(
        functools.partial(_ffn_kernel, tiles_per_seq=tiles_per_seq, final_norm=final_norm),
        grid=(n // TM,),
        in_specs=[pl.BlockSpec((HALO, D_MODEL), lambda i: (jnp.maximum(i * hb - 1, 0), 0)),
                  pl.BlockSpec((TM, D_MODEL), lambda i: (i, 0)),
                  pl.BlockSpec((HALO, D_MODEL), lambda i: (jnp.minimum((i + 1) * hb, nh - 1), 0)),
                  _const_spec((1, D_MODEL)), _const_spec(wup.shape), _const_spec(cw.shape),
                  _const_spec(cb.shape), _const_spec(wdn.shape), _const_spec((1, D_MODEL))],
        out_specs=pl.BlockSpec((TM, D_MODEL), lambda i: (i, 0)),
        out_shape=jax.ShapeDtypeStruct((n, D_MODEL), F32),
        compiler_params=_params("parallel"),
        name="conv_ffn",
    )(x2, x2, x2, g, wup, cw, cb, wdn, gf)


def kernel(x, mem, norm_mix, w_in, na_rpb, gqa_q_norm, gqa_k_norm, w_out, norm_mem_q, norm_mem_kv,
           w_mem_q, w_mem_kv, w_mem_o, norm_ffn, w_up, conv_w, conv_b, w_down, norm_final):
    batch, seq, d = x.shape
    depth = w_in.shape[0]
    mem_len = mem.shape[1]
    rows = seq // GRID_W
    assert d == D_MODEL and seq % GRID_W == 0 and rows >= NA_KH and rows % NA_ROWS_PER_STEP == 0
    assert seq % TM == 0 and seq % TM_PROJ == 0 and seq % GQA_TQ == 0 and (batch * mem_len) % TM == 0

    cos, sin = _rope_tables(seq)
    x2 = x.reshape(batch * seq, d)
    mem2 = mem.reshape(batch * mem_len, d)
    row = lambda a: a.reshape(1, -1)
    lanes2 = lambda a: jnp.tile(a, LANES // HEAD_DIM).reshape(1, LANES)

    for l in range(depth):
        naq, nak, nav, gq, gk, gv = _in_proj(x2, row(norm_mix[l]), w_in[l].astype(BF16))
        na = _na_attention(naq, nak, nav, _na_bias_table(na_rpb[l]), batch, seq)
        ga = _gqa_attention(gq, gk, gv, cos, sin, lanes2(gqa_q_norm[l]), lanes2(gqa_k_norm[l]), batch, seq)
        kv = _mem_kv(mem2, row(norm_mem_kv[l]), w_mem_kv[l].astype(BF16))
        x2 = _post(x2, na, ga, w_out[l].astype(BF16), row(norm_mem_q[l]), w_mem_q[l].astype(BF16), kv,
                   w_mem_o[l].astype(BF16), seq, mem_len)
        x2 = _ffn(x2, row(norm_ffn[l]), w_up[l].astype(BF16), conv_w[l], row(conv_b[l]),
                  w_down[l].astype(BF16), row(norm_final), seq, final_norm=(l == depth - 1))
    return x2.reshape(batch, seq, d)
```

```python
import functools

import jax
import jax.numpy as jnp
from jax import lax
from jax.experimental import pallas as pl
from jax.experimental.pallas import tpu as pltpu

D_MODEL = 1024
GRID_W = 64
HEAD_DIM = 64
NA_HEADS = 8
GQA_Q_HEADS = 8
GQA_KV_HEADS = 2
NA_KH = 8
NA_KW = 16
ROPE_THETA = 10000.0
NA_WIDTH = NA_HEADS * HEAD_DIM
GQA_WIDTH = GQA_Q_HEADS * HEAD_DIM
KV_WIDTH = GQA_KV_HEADS * HEAD_DIM
MEM_HEADS = 4
MEM_HEAD_DIM = 128
MEM_WIDTH = MEM_HEADS * MEM_HEAD_DIM
D_FF = 2816
EPS = 1e-6
NEG_INF = -1e30

LANES = 128
VMEM_LIMIT = 56 * 1024 * 1024

TM = 512
TM_PROJ = 1024
HALO = 8
MXU_TILE = 256
FF_CHUNKS = ((0, 6 * MXU_TILE), (6 * MXU_TILE, D_FF - 6 * MXU_TILE))
GQA_TQ = 1024
GQA_UNIT = 64
GQA_ONES_ROWS = 16
GQA_KEY_CHUNK = 512
GQA_KEY_TILE = 256
GQA_CHAINS = 2
GQA_SAFE_LOGIT = 100.0
GQA_BOUND_SLACK = 1.02
NA_ROWS_PER_STEP = 8
LOG2_E = 1.4426950408889634

BF16 = jnp.bfloat16
F32 = jnp.float32


def _rms(x, g):
    return x * lax.rsqrt(jnp.mean(x * x, axis=-1, keepdims=True) + EPS) * g


def _dot(a, b):
    return jnp.dot(a, b, preferred_element_type=F32)


def _dot_nt(a, b):
    return lax.dot_general(a, b, (((1,), (1,)), ((), ())), preferred_element_type=F32)


def _const_spec(shape):
    nd = len(shape)
    return pl.BlockSpec(shape, lambda *_: (0,) * nd, pipeline_mode=pl.Buffered(1))


def _params(*sem):
    return pltpu.CompilerParams(dimension_semantics=sem, vmem_limit_bytes=VMEM_LIMIT)


def _in_proj_kernel(x_ref, g_ref, w_ref, naq_ref, nak_ref, nav_ref, gq_ref, gk_ref, gv_ref):
    h = _rms(x_ref[...], g_ref[...]).astype(BF16)
    o = 0
    naq_ref[...] = (_dot(h, w_ref[:, o:o + NA_WIDTH]) * (HEAD_DIM ** -0.5 * LOG2_E)).astype(BF16)
    o += NA_WIDTH
    nak_ref[...] = _dot(h, w_ref[:, o:o + NA_WIDTH]).astype(BF16)
    o += NA_WIDTH
    nav_ref[...] = _dot(h, w_ref[:, o:o + NA_WIDTH]).astype(BF16)
    o += NA_WIDTH
    gq_ref[...] = _dot(h, w_ref[:, o:o + GQA_WIDTH])
    o += GQA_WIDTH
    gkv = _dot(h, w_ref[:, o:o + 2 * KV_WIDTH])
    gk_ref[...] = gkv[:, :KV_WIDTH]
    gv_ref[...] = gkv[:, KV_WIDTH:].astype(BF16)


def _in_proj(x2, g, w):
    n = x2.shape[0]
    in_width = w.shape[1]
    row = lambda width: pl.BlockSpec((TM_PROJ, width), lambda i: (i, 0))
    return pl.pallas_call(
        _in_proj_kernel,
        grid=(n // TM_PROJ,),
        in_specs=[row(D_MODEL), _const_spec((1, D_MODEL)), _const_spec((D_MODEL, in_width))],
        out_specs=[row(NA_WIDTH), row(NA_WIDTH), row(NA_WIDTH), row(GQA_WIDTH), row(KV_WIDTH), row(KV_WIDTH)],
        out_shape=[
            jax.ShapeDtypeStruct((n, NA_WIDTH), BF16),
            jax.ShapeDtypeStruct((n, NA_WIDTH), BF16),
            jax.ShapeDtypeStruct((n, NA_WIDTH), BF16),
            jax.ShapeDtypeStruct((n, GQA_WIDTH), F32),
            jax.ShapeDtypeStruct((n, KV_WIDTH), F32),
            jax.ShapeDtypeStruct((n, KV_WIDTH), BF16),
        ],
        compiler_params=_params("parallel"),
        name="in_proj",
    )(x2, g, w)


N_ROW_OFF = 2 * NA_KH - 1
N_COL_OFF = 2 * NA_KW - 1


def _na_bias_kernel(rpb_ref, o_ref, t_ref):
    cls = pl.program_id(0)
    lane = lax.broadcasted_iota(jnp.int32, (GRID_W, LANES), 1)

    @pl.when(cls == 0)
    def _():
        q = lax.broadcasted_iota(jnp.int32, (GRID_W, LANES), 0)
        k = lane % GRID_W
        col_off = jnp.clip(k - q, -(NA_KW - 1), NA_KW - 1) + (NA_KW - 1)
        col_start = jnp.clip(q - NA_KW // 2, 0, GRID_W - NA_KW)
        valid = (k >= col_start) & (k < col_start + NA_KW)

        def block(b, carry):
            t = jnp.full((GRID_W, LANES), NEG_INF, F32)
            for d in range(N_COL_OFF):
                t = jnp.where(valid & (col_off == d), rpb_ref[b * N_COL_OFF + d], t)
            t_ref[b] = t * LOG2_E
            return carry

        lax.fori_loop(0, NA_HEADS * N_ROW_OFF, block, 0)

    lo = lane < GRID_W
    for h in range(NA_HEADS):
        for ip in range(NA_KH // 2):
            ro = 2 * ip - cls + (NA_KH - 1)
            blk = jnp.where(lo, t_ref[h * N_ROW_OFF + ro], t_ref[h * N_ROW_OFF + ro + 1])
            o_ref[0, h // 2, (h % 2) * GRID_W:(h % 2 + 1) * GRID_W, ip * LANES:(ip + 1) * LANES] = blk


def _na_bias_table(rpb):
    shape = (NA_KH, NA_HEADS // 2, 2 * GRID_W, NA_KH * GRID_W)
    return pl.pallas_call(
        _na_bias_kernel,
        grid=(NA_KH,),
        in_specs=[pl.BlockSpec(memory_space=pltpu.SMEM)],
        out_specs=pl.BlockSpec((1,) + shape[1:], lambda c: (c, 0, 0, 0)),
        out_shape=jax.ShapeDtypeStruct(shape, F32),
        scratch_shapes=[pltpu.VMEM((NA_HEADS * N_ROW_OFF, GRID_W, LANES), F32)],
        compiler_params=_params("arbitrary"),
        name="na_bias_table",
    )(rpb.reshape(-1))


def _na_kernel(q_ref, k_ref, v_ref, bias_ref, o_ref, *, rows):
    rb = pl.program_id(1)
    lo = lax.broadcasted_iota(jnp.int32, (GRID_W, LANES), 1) < HEAD_DIM
    win = NA_KH * GRID_W
    pairs = [slice(j * LANES, (j + 1) * LANES) for j in range(NA_HEADS // 2)]
    ones = jnp.ones((win, LANES), BF16)

    def window(i):
        r = rb * NA_ROWS_PER_STEP + i
        row_start = jnp.clip(r - NA_KH // 2, 0, rows - NA_KH)
        return r - row_start, pl.multiple_of(row_start * GRID_W, GRID_W)

    def scores(i):
        cls, kstart = window(i)
        out = []
        for j, cols in enumerate(pairs):
            qp = q_ref[i * GRID_W:(i + 1) * GRID_W, cols]
            zero = jnp.zeros_like(qp)
            qs = jnp.concatenate([jnp.where(lo, qp, zero), jnp.where(lo, zero, qp)], axis=0)
            out.append(_dot_nt(qs, k_ref[pl.ds(kstart, win), cols]) + bias_ref[cls, j])
        return out

    def finish(i, s_list):
        _, kstart = window(i)
        p_list = [jnp.exp2(s - jnp.max(s, axis=-1, keepdims=True)).astype(BF16) for s in s_list]
        for p, cols in zip(p_list, pairs):
            o = _dot(p, jnp.concatenate([v_ref[pl.ds(kstart, win), cols], ones], axis=1))
            o = o[:, :LANES] / o[:, LANES:]
            o_ref[i * GRID_W:(i + 1) * GRID_W, cols] = jnp.where(lo, o[:GRID_W], o[GRID_W:]).astype(BF16)

    s_next = scores(0)
    for i in range(NA_ROWS_PER_STEP):
        s_cur = s_next
        if i + 1 < NA_ROWS_PER_STEP:
            s_next = scores(i + 1)
        finish(i, s_cur)


def _na_attention(q, k, v, bias, batch, seq):
    rows = seq // GRID_W
    tokens = NA_ROWS_PER_STEP * GRID_W
    steps = rows // NA_ROWS_PER_STEP
    qspec = pl.BlockSpec((tokens, NA_WIDTH), lambda b, r: (b * steps + r, 0))
    kvspec = pl.BlockSpec((seq, NA_WIDTH), lambda b, r: (b, 0))
    return pl.pallas_call(
        functools.partial(_na_kernel, rows=rows),
        grid=(batch, steps),
        in_specs=[qspec, kvspec, kvspec, _const_spec(bias.shape)],
        out_specs=qspec,
        out_shape=jax.ShapeDtypeStruct(q.shape, BF16),
        compiler_params=_params("parallel", "arbitrary"),
        name="na_attention",
    )(q, k, v, bias)


def _rope_tables(seq):
    t = jnp.arange(seq)
    pos = jnp.stack([t // GRID_W, t % GRID_W], axis=-1).astype(F32)
    n_f = HEAD_DIM // 4
    inv_freq = ROPE_THETA ** (-jnp.arange(n_f, dtype=F32) / n_f)
    ang = pos[:, :, None] * inv_freq
    cos, sin = jnp.cos(ang), jnp.sin(ang)
    cos_h = jnp.concatenate([cos, cos], axis=-1).reshape(seq, HEAD_DIM)
    sin_h = jnp.concatenate([-sin, sin], axis=-1).reshape(seq, HEAD_DIM)
    return jnp.tile(cos_h, (1, LANES // HEAD_DIM)), jnp.tile(sin_h, (1, LANES // HEAD_DIM))


def _norm_rope(x, gain, cos, sin, lo, first_half):
    x2 = x * x
    zero = jnp.zeros_like(x2)
    ms_lo = jnp.sum(jnp.where(lo, x2, zero), axis=-1, keepdims=True) * (1.0 / HEAD_DIM)
    ms_hi = jnp.sum(jnp.where(lo, zero, x2), axis=-1, keepdims=True) * (1.0 / HEAD_DIM)
    r = jnp.where(lo, lax.rsqrt(ms_lo + EPS), lax.rsqrt(ms_hi + EPS))
    y = x * r * gain
    n_f = HEAD_DIM // 4
    partner = jnp.where(first_half, pltpu.roll(y, LANES - n_f, 1), pltpu.roll(y, n_f, 1))
    return y * cos + partner * sin


def _gqa_kernel(q_ref, k_ref, v_ref, cos_ref, sin_ref, qg_ref, kg_ref, o_ref, kn_ref, vx_ref, *s_refs):
    qi = pl.program_id(1)
    seq = k_ref.shape[0]
    n_units = GQA_TQ // GQA_UNIT
    n_chunks = seq // GQA_KEY_CHUNK
    heads_per_group = GQA_Q_HEADS // GQA_KV_HEADS
    stacked_rows = GQA_Q_HEADS * GQA_UNIT
    group_cols = heads_per_group * GQA_UNIT
    vx_rows = HEAD_DIM + GQA_ONES_ROWS

    @pl.when(qi == 0)
    def _():
        lane_k = lax.broadcasted_iota(jnp.int32, (seq, LANES), 1)
        kn = _norm_rope(k_ref[...], kg_ref[...], cos_ref[...], sin_ref[...],
                        lane_k < HEAD_DIM, (lane_k % (HEAD_DIM // 2)) < HEAD_DIM // 4)
        kn_ref[...] = kn.astype(BF16)
        vt = v_ref[...].astype(F32).T.astype(BF16)
        for g in range(GQA_KV_HEADS):
            vx_ref[g * vx_rows:g * vx_rows + HEAD_DIM, :] = vt[g * HEAD_DIM:(g + 1) * HEAD_DIM]
            vx_ref[g * vx_rows + HEAD_DIM:(g + 1) * vx_rows, :] = jnp.ones((GQA_ONES_ROWS, seq), BF16)

    lane = lax.broadcasted_iota(jnp.int32, (GQA_UNIT, LANES), 1)
    lo = lane < HEAD_DIM
    first_half = (lane % (HEAD_DIM // 2)) < HEAD_DIM // 4
    zero = jnp.zeros((GQA_UNIT, LANES), F32)
    scale = HEAD_DIM ** -0.5 * LOG2_E

    def prep_q(u):
        r0 = pl.multiple_of(u * GQA_UNIT, GQA_UNIT)
        t0 = pl.multiple_of(qi * GQA_TQ + u * GQA_UNIT, GQA_UNIT)
        cos = cos_ref[pl.ds(t0, GQA_UNIT), :]
        sin = sin_ref[pl.ds(t0, GQA_UNIT), :]
        stacked = []
        for pair in range(GQA_Q_HEADS // 2):
            qp = q_ref[pl.ds(r0, GQA_UNIT), pair * LANES:(pair + 1) * LANES]
            qn = _norm_rope(qp, qg_ref[...], cos, sin, lo, first_half) * scale
            sw = pltpu.roll(qn, HEAD_DIM, 1)
            if pair < heads_per_group // 2:
                stacked += [jnp.where(lo, qn, zero), jnp.where(lo, sw, zero)]
            else:
                stacked += [jnp.where(lo, zero, sw), jnp.where(lo, zero, qn)]
        return jnp.concatenate(stacked, axis=0).T.astype(BF16)

    def s_scr(k, c):
        return s_refs[2 * k + c % 2]

    def chunk_scores(c, qt, k, with_max=True):
        s = _dot(kn_ref[c * GQA_KEY_CHUNK:(c + 1) * GQA_KEY_CHUNK, :], qt)
        s_scr(k, c)[...] = s
        return jnp.max(s, axis=0, keepdims=True) if with_max else None

    def pv_dot(keys, p):
        return tuple(_dot(vx_ref[g * vx_rows:(g + 1) * vx_rows, keys], p[:, g * group_cols:(g + 1) * group_cols])
                     for g in range(GQA_KV_HEADS))

    def write_out(u, acc):
        r0 = pl.multiple_of(u * GQA_UNIT, GQA_UNIT)
        blank = jnp.zeros((HEAD_DIM, group_cols), F32)
        o = jnp.concatenate(
            [jnp.concatenate([a[:HEAD_DIM] / a[HEAD_DIM:HEAD_DIM + 1] if g == gg else blank
                              for gg in range(GQA_KV_HEADS)], axis=1)
             for g, a in enumerate(acc)], axis=0).T
        for pair in range(GQA_Q_HEADS // 2):
            even = o[(2 * pair) * GQA_UNIT:(2 * pair + 1) * GQA_UNIT]
            odd = o[(2 * pair + 1) * GQA_UNIT:(2 * pair + 2) * GQA_UNIT]
            if pair < heads_per_group // 2:
                out = jnp.where(lo, even, pltpu.roll(odd, HEAD_DIM, 1))
            else:
                out = jnp.where(lo, pltpu.roll(even, HEAD_DIM, 1), odd)
            o_ref[pl.ds(r0, GQA_UNIT), pair * LANES:(pair + 1) * LANES] = out.astype(BF16)

    chains = range(GQA_CHAINS)

    def trip(i, carry):
        qt, cmax, acc_prev = [list(x) for x in carry]
        qt_next = [None] * GQA_CHAINS
        units = [i * GQA_CHAINS + k for k in chains]
        m = [jnp.full((1, stacked_rows), NEG_INF, F32) for _ in chains]
        acc = [tuple(jnp.zeros((vx_rows, group_cols), F32) for _ in range(GQA_KV_HEADS)) for _ in chains]
        for c in range(n_chunks):
            for k in chains:
                m_new = jnp.maximum(m[k], cmax[k])
                if c + 1 < n_chunks:
                    cmax[k] = chunk_scores(c + 1, qt[k], k)
                else:
                    cmax[k] = chunk_scores(0, qt_next[k], k)
                if c == 0:
                    write_out(jnp.maximum(units[k] - GQA_CHAINS, k), acc_prev[k])
                if c == 1:
                    qt_next[k] = prep_q(jnp.minimum(units[k] + GQA_CHAINS, n_units - GQA_CHAINS + k))
                pv = None
                for t in range(GQA_KEY_CHUNK // GQA_KEY_TILE):
                    rows = slice(t * GQA_KEY_TILE, (t + 1) * GQA_KEY_TILE)
                    k0 = c * GQA_KEY_CHUNK + t * GQA_KEY_TILE
                    p = jnp.exp2(s_scr(k, c)[rows, :] - m_new).astype(BF16)
                    d = pv_dot(slice(k0, k0 + GQA_KEY_TILE), p)
                    pv = d if pv is None else tuple(a + b for a, b in zip(pv, d))
                alpha = jnp.exp2(m[k] - m_new)
                acc[k] = tuple(a * alpha[:, g * group_cols:(g + 1) * group_cols] + b
                               for g, (a, b) in enumerate(zip(acc[k], pv)))
                m[k] = m_new
        return tuple(qt_next), tuple(cmax), tuple(acc)

    def trip_bounded(i, carry):
        qt, acc_prev = [list(x) for x in carry]
        qt_next = [None] * GQA_CHAINS
        units = [i * GQA_CHAINS + k for k in chains]
        acc = [None] * GQA_CHAINS
        for c in range(n_chunks):
            rows = slice(c * GQA_KEY_CHUNK, (c + 1) * GQA_KEY_CHUNK)
            for k in chains:
                if c + 1 < n_chunks:
                    chunk_scores(c + 1, qt[k], k, with_max=False)
                else:
                    chunk_scores(0, qt_next[k], k, with_max=False)
                if c == 0:
                    write_out(jnp.maximum(units[k] - GQA_CHAINS, k), acc_prev[k])
                if c == 1:
                    qt_next[k] = prep_q(jnp.minimum(units[k] + GQA_CHAINS, n_units - GQA_CHAINS + k))
                d = pv_dot(rows, jnp.exp2(s_scr(k, c)[...]).astype(BF16))
                acc[k] = d if acc[k] is None else tuple(a + b for a, b in zip(acc[k], d))
        return tuple(qt_next), tuple(acc)

    placeholder = tuple(tuple(jnp.ones((vx_rows, group_cols), F32) for _ in range(GQA_KV_HEADS)) for _ in chains)

    def run_exact():
        assert n_chunks % 2 == 0
        qt0 = tuple(prep_q(k) for k in chains)
        init = (qt0, tuple(chunk_scores(0, qt0[k], k) for k in chains), placeholder)
        _, _, acc_last = lax.fori_loop(0, n_units // GQA_CHAINS, trip, init)
        for k in chains:
            write_out(n_units - GQA_CHAINS + k, acc_last[k])

    def run_bounded():
        qt0 = tuple(prep_q(k) for k in chains)
        for k in chains:
            chunk_scores(0, qt0[k], k, with_max=False)
        init = (qt0, placeholder)
        _, acc_last = lax.fori_loop(0, n_units // GQA_CHAINS, trip_bounded, init)
        for k in chains:
            write_out(n_units - GQA_CHAINS + k, acc_last[k])

    logit_bound = (HEAD_DIM * scale * GQA_BOUND_SLACK) * jnp.max(jnp.abs(qg_ref[...])) * jnp.max(jnp.abs(kg_ref[...]))
    lax.cond(logit_bound <= GQA_SAFE_LOGIT, run_bounded, run_exact)


def _gqa_attention(q, k, v, cos, sin, qg, kg, batch, seq):
    steps = seq // GQA_TQ
    qspec = pl.BlockSpec((GQA_TQ, GQA_WIDTH), lambda b, i: (b * steps + i, 0))
    kvspec = pl.BlockSpec((seq, KV_WIDTH), lambda b, i: (b, 0))
    return pl.pallas_call(
        _gqa_kernel,
        grid=(batch, steps),
        in_specs=[qspec, kvspec, kvspec, _const_spec((seq, LANES)), _const_spec((seq, LANES)),
                  _const_spec((1, LANES)), _const_spec((1, LANES))],
        out_specs=qspec,
        out_shape=jax.ShapeDtypeStruct(q.shape, BF16),
        scratch_shapes=[pltpu.VMEM((seq, KV_WIDTH), BF16),
                        pltpu.VMEM((GQA_KV_HEADS * (HEAD_DIM + GQA_ONES_ROWS), seq), BF16)]
        + [pltpu.VMEM((GQA_KEY_CHUNK, GQA_Q_HEADS * GQA_UNIT), F32)] * (2 * GQA_CHAINS),
        compiler_params=_params("parallel", "arbitrary"),
        name="gqa_attention",
    )(q, k, v, cos, sin, qg, kg)


def _mem_kv_kernel(m_ref, g_ref, w_ref, o_ref):
    h = _rms(m_ref[...], g_ref[...]).astype(BF16)
    o_ref[...] = _dot(h, w_ref[...]).astype(BF16)


def _mem_kv(mem2, g, w):
    n = mem2.shape[0]
    return pl.pallas_call(
        _mem_kv_kernel,
        grid=(n // TM,),
        in_specs=[pl.BlockSpec((TM, D_MODEL), lambda i: (i, 0)), _const_spec((1, D_MODEL)),
                  _const_spec(w.shape)],
        out_specs=pl.BlockSpec((TM, w.shape[1]), lambda i: (i, 0)),
        out_shape=jax.ShapeDtypeStruct((n, w.shape[1]), BF16),
        compiler_params=_params("parallel"),
        name="mem_kv",
    )(mem2, g, w)


def _post_kernel(x_ref, na_ref, ga_ref, wo_ref, g_ref, wq_ref, kv_ref, wmo_ref, o_ref):
    x1 = x_ref[...] + _dot(na_ref[...], wo_ref[:NA_WIDTH, :]) + _dot(ga_ref[...], wo_ref[NA_WIDTH:, :])
    h = _rms(x1, g_ref[...]).astype(BF16)
    q = (_dot(h, wq_ref[...]) * (MEM_HEAD_DIM ** -0.5)).astype(BF16)
    heads = []
    for hh in range(MEM_HEADS):
        cols = slice(hh * MEM_HEAD_DIM, (hh + 1) * MEM_HEAD_DIM)
        s = _dot_nt(q[:, cols], kv_ref[:, cols])
        m = jnp.max(s, axis=-1, keepdims=True)
        p = jnp.exp(s - m)
        l = jnp.sum(p, axis=-1, keepdims=True)
        vcols = slice(MEM_WIDTH + hh * MEM_HEAD_DIM, MEM_WIDTH + (hh + 1) * MEM_HEAD_DIM)
        heads.append((_dot(p.astype(BF16), kv_ref[:, vcols]) / l).astype(BF16))
    o_ref[...] = x1 + _dot(jnp.concatenate(heads, axis=-1), wmo_ref[...])


def _post(x2, na, ga, wo, g, wq, kv, wmo, seq, mem_len):
    n = x2.shape[0]
    tiles_per_seq = seq // TM_PROJ
    row = lambda width: pl.BlockSpec((TM_PROJ, width), lambda i: (i, 0))
    return pl.pallas_call(
        _post_kernel,
        grid=(n // TM_PROJ,),
        in_specs=[row(D_MODEL), row(NA_WIDTH), row(GQA_WIDTH), _const_spec(wo.shape),
                  _const_spec((1, D_MODEL)), _const_spec(wq.shape),
                  pl.BlockSpec((mem_len, 2 * MEM_WIDTH), lambda i: (i // tiles_per_seq, 0)),
                  _const_spec(wmo.shape)],
        out_specs=row(D_MODEL),
        out_shape=jax.ShapeDtypeStruct((n, D_MODEL), F32),
        compiler_params=_params("parallel"),
        name="post_mixer_memattn",
    )(x2, na, ga, wo, g, wq, kv, wmo)


def _ffn_kernel(xp_ref, x_ref, xn_ref, g_ref, wup_ref, cw_ref, cb_ref, wdn_ref, gf_ref, o_ref,
                *, tiles_per_seq, final_norm):
    i = pl.program_id(0)
    first = (i % tiles_per_seq) == 0
    last = (i % tiles_per_seq) == tiles_per_seq - 1
    x = x_ref[...]
    tm = x.shape[0]
    xa = jnp.concatenate([xp_ref[...], x, xn_ref[...]], axis=0)
    ha = _rms(xa, g_ref[...])
    n_all = tm + 2 * HALO
    rid = lax.broadcasted_iota(jnp.int32, (n_all, 1), 0)
    pad = (first & (rid < HALO)) | (last & (rid >= tm + HALO))
    ha = jnp.where(pad, 0.0, ha).astype(BF16)

    def conv(u, c0, width):
        cols = slice(c0, c0 + width)
        prev = pltpu.roll(u, 1, 0)[HALO:HALO + tm]
        nxt = pltpu.roll(u, n_all - 1, 0)[HALO:HALO + tm]
        return (prev * cw_ref[0:1, cols] + u[HALO:HALO + tm] * cw_ref[1:2, cols]
                + nxt * cw_ref[2:3, cols] + cb_ref[:, cols])

    def up(c):
        c0, width = FF_CHUNKS[c]
        return (_dot(ha, wup_ref[:, c0:c0 + width]), _dot(ha, wup_ref[:, D_FF + c0:D_FF + c0 + width]))

    acc = x
    u_next = up(0)
    for c, (c0, width) in enumerate(FF_CHUNKS):
        ug, uv = u_next
        if c + 1 < len(FF_CHUNKS):
            u_next = up(c + 1)
        gate = conv(ug, c0, width)
        val = conv(uv, D_FF + c0, width)
        act = (gate * jax.nn.sigmoid(gate) * val).astype(BF16)
        acc = acc + _dot(act, wdn_ref[c0:c0 + width, :])
    if final_norm:
        acc = _rms(acc, gf_ref[...])
    o_ref[...] = acc


def _ffn(x2, g, wup, cw, cb, wdn, gf, seq, final_norm):
    n = x2.shape[0]
    tiles_per_seq = seq // TM_PROJ
    hb = TM_PROJ // HALO
    nh = n // HALO
    return pl.pallas_call(
        functools.partial(_ffn_kernel, tiles_per_seq=tiles_per_seq, final_norm=final_norm),
        grid=(n // TM_PROJ,),
        in_specs=[pl.BlockSpec((HALO, D_MODEL), lambda i: (jnp.maximum(i * hb - 1, 0), 0)),
                  pl.BlockSpec((TM_PROJ, D_MODEL), lambda i: (i, 0)),
                  pl.BlockSpec((HALO, D_MODEL), lambda i: (jnp.minimum((i + 1) * hb, nh - 1), 0)),
                  _const_spec((1, D_MODEL)), _const_spec(wup.shape), _const_spec(cw.shape),
                  _const_spec(cb.shape), _const_spec(wdn.shape), _const_spec((1, D_MODEL))],
        out_specs=pl.BlockSpec((TM_PROJ, D_MODEL), lambda i: (i, 0)),
        out_shape=jax.ShapeDtypeStruct((n, D_MODEL), F32),
        compiler_params=_params("parallel"),
        name="conv_ffn",
    )(x2, x2, x2, g, wup, cw, cb, wdn, gf)


def kernel(x, mem, norm_mix, w_in, na_rpb, gqa_q_norm, gqa_k_norm, w_out, norm_mem_q, norm_mem_kv,
           w_mem_q, w_mem_kv, w_mem_o, norm_ffn, w_up, conv_w, conv_b, w_down, norm_final):
    batch, seq, d = x.shape
    depth = w_in.shape[0]
    mem_len = mem.shape[1]
    rows = seq // GRID_W
    assert d == D_MODEL and seq % GRID_W == 0 and rows >= NA_KH and rows % NA_ROWS_PER_STEP == 0
    assert seq % TM == 0 and seq % TM_PROJ == 0 and seq % GQA_TQ == 0 and (batch * mem_len) % TM == 0

    cos, sin = _rope_tables(seq)
    x2 = x.reshape(batch * seq, d)
    mem2 = mem.reshape(batch * mem_len, d)
    row = lambda a: a.reshape(1, -1)
    lanes2 = lambda a: jnp.tile(a, LANES // HEAD_DIM).reshape(1, LANES)

    for l in range(depth):
        naq, nak, nav, gq, gk, gv = _in_proj(x2, row(norm_mix[l]), w_in[l].astype(BF16))
        na = _na_attention(naq, nak, nav, _na_bias_table(na_rpb[l]), batch, seq)
        ga = _gqa_attention(gq, gk, gv, cos, sin, lanes2(gqa_q_norm[l]), lanes2(gqa_k_norm[l]), batch, seq)
        kv = _mem_kv(mem2, row(norm_mem_kv[l]), w_mem_kv[l].astype(BF16))
        x2 = _post(x2, na, ga, w_out[l].astype(BF16), row(norm_mem_q[l]), w_mem_q[l].astype(BF16), kv,
                   w_mem_o[l].astype(BF16), seq, mem_len)
        x2 = _ffn(x2, row(norm_ffn[l]), w_up[l].astype(BF16), conv_w[l], row(conv_b[l]),
                  w_down[l].astype(BF16), row(norm_final), seq, final_norm=(l == depth - 1))
    return x2.reshape(batch, seq, d)
```

```python
import functools

import jax
import jax.numpy as jnp
from jax import lax
from jax.experimental import pallas as pl
from jax.experimental.pallas import tpu as pltpu

D_MODEL = 1024
GRID_W = 64
HEAD_DIM = 64
NA_HEADS = 8
GQA_Q_HEADS = 8
GQA_KV_HEADS = 2
NA_KH = 8
NA_KW = 16
ROPE_THETA = 10000.0
NA_WIDTH = NA_HEADS * HEAD_DIM
GQA_WIDTH = GQA_Q_HEADS * HEAD_DIM
KV_WIDTH = GQA_KV_HEADS * HEAD_DIM
MEM_HEADS = 4
MEM_HEAD_DIM = 128
MEM_WIDTH = MEM_HEADS * MEM_HEAD_DIM
D_FF = 2816
EPS = 1e-6
NEG_INF = -1e30

LANES = 128
VMEM_LIMIT = 56 * 1024 * 1024

TM = 512
TM_PROJ = 1024
HALO = 8
MXU_TILE = 256
FF_CHUNKS = ((0, 6 * MXU_TILE), (6 * MXU_TILE, D_FF - 6 * MXU_TILE))
GQA_TQ = 1024
GQA_UNIT = 64
GQA_ONES_ROWS = 16
GQA_KEY_CHUNK = 512
GQA_KEY_TILE = 256
GQA_CHAINS = 2
GQA_SAFE_LOGIT = 100.0
GQA_BOUND_SLACK = 1.02
NA_ROWS_PER_STEP = 8
LOG2_E = 1.4426950408889634

BF16 = jnp.bfloat16
F32 = jnp.float32


def _rms(x, g):
    return x * lax.rsqrt(jnp.mean(x * x, axis=-1, keepdims=True) + EPS) * g


def _dot(a, b):
    return jnp.dot(a, b, preferred_element_type=F32)


def _dot_nt(a, b):
    return lax.dot_general(a, b, (((1,), (1,)), ((), ())), preferred_element_type=F32)


def _const_spec(shape):
    nd = len(shape)
    return pl.BlockSpec(shape, lambda *_: (0,) * nd, pipeline_mode=pl.Buffered(1))


def _layer_spec(stacked, l):
    tail = stacked.shape[1:]
    return pl.BlockSpec((None,) + tail, lambda *_: (l,) + (0,) * len(tail), pipeline_mode=pl.Buffered(1))


def _params(*sem):
    return pltpu.CompilerParams(dimension_semantics=sem, vmem_limit_bytes=VMEM_LIMIT)


def _in_proj_kernel(x_ref, g_ref, w_ref, naq_ref, nak_ref, nav_ref, gq_ref, gk_ref, gv_ref):
    h = _rms(x_ref[...], g_ref[...]).astype(BF16)
    o = 0
    naq_ref[...] = (_dot(h, w_ref[:, o:o + NA_WIDTH]) * (HEAD_DIM ** -0.5 * LOG2_E)).astype(BF16)
    o += NA_WIDTH
    nak_ref[...] = _dot(h, w_ref[:, o:o + NA_WIDTH]).astype(BF16)
    o += NA_WIDTH
    nav_ref[...] = _dot(h, w_ref[:, o:o + NA_WIDTH]).astype(BF16)
    o += NA_WIDTH
    gq_ref[...] = _dot(h, w_ref[:, o:o + GQA_WIDTH])
    o += GQA_WIDTH
    gkv = _dot(h, w_ref[:, o:o + 2 * KV_WIDTH])
    gk_ref[...] = gkv[:, :KV_WIDTH]
    gv_ref[...] = gkv[:, KV_WIDTH:].astype(BF16)


def _in_proj(x2, g, w, l):
    n = x2.shape[0]
    row = lambda width: pl.BlockSpec((TM_PROJ, width), lambda i: (i, 0))
    return pl.pallas_call(
        _in_proj_kernel,
        grid=(n // TM_PROJ,),
        in_specs=[row(D_MODEL), _layer_spec(g, l), _layer_spec(w, l)],
        out_specs=[row(NA_WIDTH), row(NA_WIDTH), row(NA_WIDTH), row(GQA_WIDTH), row(KV_WIDTH), row(KV_WIDTH)],
        out_shape=[
            jax.ShapeDtypeStruct((n, NA_WIDTH), BF16),
            jax.ShapeDtypeStruct((n, NA_WIDTH), BF16),
            jax.ShapeDtypeStruct((n, NA_WIDTH), BF16),
            jax.ShapeDtypeStruct((n, GQA_WIDTH), F32),
            jax.ShapeDtypeStruct((n, KV_WIDTH), F32),
            jax.ShapeDtypeStruct((n, KV_WIDTH), BF16),
        ],
        compiler_params=_params("parallel"),
        name="in_proj",
    )(x2, g, w)


N_ROW_OFF = 2 * NA_KH - 1
N_COL_OFF = 2 * NA_KW - 1


def _na_bias_kernel(rpb_ref, o_ref, t_ref):
    cls = pl.program_id(0)
    lane = lax.broadcasted_iota(jnp.int32, (GRID_W, LANES), 1)

    @pl.when(cls == 0)
    def _():
        q = lax.broadcasted_iota(jnp.int32, (GRID_W, LANES), 0)
        k = lane % GRID_W
        col_start = jnp.clip(q - NA_KW // 2, 0, GRID_W - NA_KW)
        valid = (k >= col_start) & (k < col_start + NA_KW)
        lane8 = lax.broadcasted_iota(jnp.int32, (8, LANES), 1)
        at = [((lane8 == (d - (NA_KW - 1)) % LANES) | (lane8 == (d - (NA_KW - 1) + GRID_W) % LANES))
              for d in range(N_COL_OFF)]

        def block(b, carry):
            base = jnp.zeros((8, LANES), F32)
            for d in range(N_COL_OFF):
                base = jnp.where(at[d], rpb_ref[b * N_COL_OFF + d], base)
            t = pltpu.roll(jnp.tile(base, (GRID_W // 8, 1)), 0, 1, stride=1, stride_axis=0)
            t_ref[b] = jnp.where(valid, t, NEG_INF) * LOG2_E
            return carry

        lax.fori_loop(0, NA_HEADS * N_ROW_OFF, block, 0)

    lo = lane < GRID_W
    for h in range(NA_HEADS):
        for ip in range(NA_KH // 2):
            ro = 2 * ip - cls + (NA_KH - 1)
            blk = jnp.where(lo, t_ref[h * N_ROW_OFF + ro], t_ref[h * N_ROW_OFF + ro + 1])
            o_ref[0, h // 2, (h % 2) * GRID_W:(h % 2 + 1) * GRID_W, ip * LANES:(ip + 1) * LANES] = blk


def _na_bias_table(rpb):
    shape = (NA_KH, NA_HEADS // 2, 2 * GRID_W, NA_KH * GRID_W)
    return pl.pallas_call(
        _na_bias_kernel,
        grid=(NA_KH,),
        in_specs=[pl.BlockSpec(memory_space=pltpu.SMEM)],
        out_specs=pl.BlockSpec((1,) + shape[1:], lambda c: (c, 0, 0, 0)),
        out_shape=jax.ShapeDtypeStruct(shape, F32),
        scratch_shapes=[pltpu.VMEM((NA_HEADS * N_ROW_OFF, GRID_W, LANES), F32)],
        compiler_params=_params("arbitrary"),
        name="na_bias_table",
    )(rpb.reshape(-1))


def _na_kernel(q_ref, k_ref, v_ref, bias_ref, o_ref, *, rows):
    rb = pl.program_id(1)
    lo = lax.broadcasted_iota(jnp.int32, (GRID_W, LANES), 1) < HEAD_DIM
    win = NA_KH * GRID_W
    pairs = [slice(j * LANES, (j + 1) * LANES) for j in range(NA_HEADS // 2)]
    ones = jnp.ones((win, LANES), BF16)

    def window(i):
        r = rb * NA_ROWS_PER_STEP + i
        row_start = jnp.clip(r - NA_KH // 2, 0, rows - NA_KH)
        return r - row_start, pl.multiple_of(row_start * GRID_W, GRID_W)

    def scores(i):
        cls, kstart = window(i)
        out = []
        for j, cols in enumerate(pairs):
            qp = q_ref[i * GRID_W:(i + 1) * GRID_W, cols]
            zero = jnp.zeros_like(qp)
            qs = jnp.concatenate([jnp.where(lo, qp, zero), jnp.where(lo, zero, qp)], axis=0)
            out.append(_dot_nt(qs, k_ref[pl.ds(kstart, win), cols]) + bias_ref[cls, j])
        return out

    def finish(i, s_list):
        _, kstart = window(i)
        p_list = [jnp.exp2(s - jnp.max(s, axis=-1, keepdims=True)).astype(BF16) for s in s_list]
        for p, cols in zip(p_list, pairs):
            o = _dot(p, jnp.concatenate([v_ref[pl.ds(kstart, win), cols], ones], axis=1))
            o = o[:, :LANES] / o[:, LANES:]
            o_ref[i * GRID_W:(i + 1) * GRID_W, cols] = jnp.where(lo, o[:GRID_W], o[GRID_W:]).astype(BF16)

    s_next = scores(0)
    for i in range(NA_ROWS_PER_STEP):
        s_cur = s_next
        if i + 1 < NA_ROWS_PER_STEP:
            s_next = scores(i + 1)
        finish(i, s_cur)


def _na_attention(q, k, v, bias, batch, seq):
    rows = seq // GRID_W
    tokens = NA_ROWS_PER_STEP * GRID_W
    steps = rows // NA_ROWS_PER_STEP
    qspec = pl.BlockSpec((tokens, NA_WIDTH), lambda b, r: (b * steps + r, 0))
    kvspec = pl.BlockSpec((seq, NA_WIDTH), lambda b, r: (b, 0))
    return pl.pallas_call(
        functools.partial(_na_kernel, rows=rows),
        grid=(batch, steps),
        in_specs=[qspec, kvspec, kvspec, _const_spec(bias.shape)],
        out_specs=qspec,
        out_shape=jax.ShapeDtypeStruct(q.shape, BF16),
        compiler_params=_params("parallel", "arbitrary"),
        name="na_attention",
    )(q, k, v, bias)


def _rope_tables(seq):
    t = jnp.arange(seq)
    pos = jnp.stack([t // GRID_W, t % GRID_W], axis=-1).astype(F32)
    n_f = HEAD_DIM // 4
    inv_freq = ROPE_THETA ** (-jnp.arange(n_f, dtype=F32) / n_f)
    ang = pos[:, :, None] * inv_freq
    cos, sin = jnp.cos(ang), jnp.sin(ang)
    cos_h = jnp.concatenate([cos, cos], axis=-1).reshape(seq, HEAD_DIM)
    sin_h = jnp.concatenate([-sin, sin], axis=-1).reshape(seq, HEAD_DIM)
    return jnp.tile(cos_h, (1, LANES // HEAD_DIM)), jnp.tile(sin_h, (1, LANES // HEAD_DIM))


def _norm_rope(x, gain, cos, sin, lo, first_half):
    x2 = x * x
    zero = jnp.zeros_like(x2)
    ms_lo = jnp.sum(jnp.where(lo, x2, zero), axis=-1, keepdims=True) * (1.0 / HEAD_DIM)
    ms_hi = jnp.sum(jnp.where(lo, zero, x2), axis=-1, keepdims=True) * (1.0 / HEAD_DIM)
    r = jnp.where(lo, lax.rsqrt(ms_lo + EPS), lax.rsqrt(ms_hi + EPS))
    y = x * r * gain
    n_f = HEAD_DIM // 4
    partner = jnp.where(first_half, pltpu.roll(y, LANES - n_f, 1), pltpu.roll(y, n_f, 1))
    return y * cos + partner * sin


def _gqa_kernel(q_ref, k_ref, v_ref, cos_ref, sin_ref, qg_ref, kg_ref, o_ref, kn_ref, vx_ref, *s_refs):
    qi = pl.program_id(1)
    seq = k_ref.shape[0]
    n_units = GQA_TQ // GQA_UNIT
    n_chunks = seq // GQA_KEY_CHUNK
    heads_per_group = GQA_Q_HEADS // GQA_KV_HEADS
    stacked_rows = GQA_Q_HEADS * GQA_UNIT

    @pl.when(qi == 0)
    def _():
        lane_k = lax.broadcasted_iota(jnp.int32, (seq, LANES), 1)
        kn = _norm_rope(k_ref[...], kg_ref[...], cos_ref[...], sin_ref[...],
                        lane_k < HEAD_DIM, (lane_k % (HEAD_DIM // 2)) < HEAD_DIM // 4)
        kn_ref[...] = kn.astype(BF16)
        vx_ref[:KV_WIDTH, :] = v_ref[...].astype(F32).T.astype(BF16)
        vx_ref[KV_WIDTH:, :] = jnp.ones((GQA_ONES_ROWS, seq), BF16)

    lane = lax.broadcasted_iota(jnp.int32, (GQA_UNIT, LANES), 1)
    lo = lane < HEAD_DIM
    first_half = (lane % (HEAD_DIM // 2)) < HEAD_DIM // 4
    zero = jnp.zeros((GQA_UNIT, LANES), F32)
    scale = HEAD_DIM ** -0.5 * LOG2_E

    def prep_q(u):
        r0 = pl.multiple_of(u * GQA_UNIT, GQA_UNIT)
        t0 = pl.multiple_of(qi * GQA_TQ + u * GQA_UNIT, GQA_UNIT)
        cos = cos_ref[pl.ds(t0, GQA_UNIT), :]
        sin = sin_ref[pl.ds(t0, GQA_UNIT), :]
        stacked = []
        for pair in range(GQA_Q_HEADS // 2):
            qp = q_ref[pl.ds(r0, GQA_UNIT), pair * LANES:(pair + 1) * LANES]
            qn = _norm_rope(qp, qg_ref[...], cos, sin, lo, first_half) * scale
            sw = pltpu.roll(qn, HEAD_DIM, 1)
            if pair < heads_per_group // 2:
                stacked += [jnp.where(lo, qn, zero), jnp.where(lo, sw, zero)]
            else:
                stacked += [jnp.where(lo, zero, sw), jnp.where(lo, zero, qn)]
        return jnp.concatenate(stacked, axis=0).T.astype(BF16)

    def s_scr(k, c):
        return s_refs[2 * k + c % 2]

    def chunk_scores(c, qt, k, with_max=True):
        s = _dot(kn_ref[c * GQA_KEY_CHUNK:(c + 1) * GQA_KEY_CHUNK, :], qt)
        s_scr(k, c)[...] = s
        return jnp.max(s, axis=0, keepdims=True) if with_max else None

    def write_out(u, acc):
        r0 = pl.multiple_of(u * GQA_UNIT, GQA_UNIT)
        o = (acc[:KV_WIDTH] / acc[KV_WIDTH:KV_WIDTH + 1]).T
        for pair in range(GQA_Q_HEADS // 2):
            even = o[(2 * pair) * GQA_UNIT:(2 * pair + 1) * GQA_UNIT]
            odd = o[(2 * pair + 1) * GQA_UNIT:(2 * pair + 2) * GQA_UNIT]
            if pair < heads_per_group // 2:
                out = jnp.where(lo, even, pltpu.roll(odd, HEAD_DIM, 1))
            else:
                out = jnp.where(lo, pltpu.roll(even, HEAD_DIM, 1), odd)
            o_ref[pl.ds(r0, GQA_UNIT), pair * LANES:(pair + 1) * LANES] = out.astype(BF16)

    chains = range(GQA_CHAINS)

    def trip(i, carry):
        qt, cmax, acc_prev = [list(x) for x in carry]
        qt_next = [None] * GQA_CHAINS
        units = [i * GQA_CHAINS + k for k in chains]
        m = [jnp.full((1, stacked_rows), NEG_INF, F32) for _ in chains]
        acc = [jnp.zeros((KV_WIDTH + GQA_ONES_ROWS, stacked_rows), F32) for _ in chains]
        for c in range(n_chunks):
            for k in chains:
                m_new = jnp.maximum(m[k], cmax[k])
                if c + 1 < n_chunks:
                    cmax[k] = chunk_scores(c + 1, qt[k], k)
                else:
                    cmax[k] = chunk_scores(0, qt_next[k], k)
                if c == 0:
                    write_out(jnp.maximum(units[k] - GQA_CHAINS, k), acc_prev[k])
                if c == 1:
                    qt_next[k] = prep_q(jnp.minimum(units[k] + GQA_CHAINS, n_units - GQA_CHAINS + k))
                pv = None
                for t in range(GQA_KEY_CHUNK // GQA_KEY_TILE):
                    rows = slice(t * GQA_KEY_TILE, (t + 1) * GQA_KEY_TILE)
                    k0 = c * GQA_KEY_CHUNK + t * GQA_KEY_TILE
                    p = jnp.exp2(s_scr(k, c)[rows, :] - m_new).astype(BF16)
                    d = _dot(vx_ref[:, k0:k0 + GQA_KEY_TILE], p)
                    pv = d if pv is None else pv + d
                acc[k] = acc[k] * jnp.exp2(m[k] - m_new) + pv
                m[k] = m_new
        return tuple(qt_next), tuple(cmax), tuple(acc)

    def trip_bounded(i, carry):
        qt, acc_prev = [list(x) for x in carry]
        qt_next = [None] * GQA_CHAINS
        units = [i * GQA_CHAINS + k for k in chains]
        acc = [None] * GQA_CHAINS
        for c in range(n_chunks):
            rows = slice(c * GQA_KEY_CHUNK, (c + 1) * GQA_KEY_CHUNK)
            for k in chains:
                if c + 1 < n_chunks:
                    chunk_scores(c + 1, qt[k], k, with_max=False)
                else:
                    chunk_scores(0, qt_next[k], k, with_max=False)
                if c == 0:
                    write_out(jnp.maximum(units[k] - GQA_CHAINS, k), acc_prev[k])
                if c == 1:
                    qt_next[k] = prep_q(jnp.minimum(units[k] + GQA_CHAINS, n_units - GQA_CHAINS + k))
                d = _dot(vx_ref[:, rows], jnp.exp2(s_scr(k, c)[...]).astype(BF16))
                acc[k] = d if acc[k] is None else acc[k] + d
        return tuple(qt_next), tuple(acc)

    placeholder = tuple(jnp.ones((KV_WIDTH + GQA_ONES_ROWS, stacked_rows), F32) for _ in chains)

    def run_exact():
        assert n_chunks % 2 == 0
        qt0 = tuple(prep_q(k) for k in chains)
        init = (qt0, tuple(chunk_scores(0, qt0[k], k) for k in chains), placeholder)
        _, _, acc_last = lax.fori_loop(0, n_units // GQA_CHAINS, trip, init)
        for k in chains:
            write_out(n_units - GQA_CHAINS + k, acc_last[k])

    def run_bounded():
        qt0 = tuple(prep_q(k) for k in chains)
        for k in chains:
            chunk_scores(0, qt0[k], k, with_max=False)
        init = (qt0, placeholder)
        _, acc_last = lax.fori_loop(0, n_units // GQA_CHAINS, trip_bounded, init)
        for k in chains:
            write_out(n_units - GQA_CHAINS + k, acc_last[k])

    logit_bound = (HEAD_DIM * scale * GQA_BOUND_SLACK) * jnp.max(jnp.abs(qg_ref[...])) * jnp.max(jnp.abs(kg_ref[...]))
    lax.cond(logit_bound <= GQA_SAFE_LOGIT, run_bounded, run_exact)


def _gqa_attention(q, k, v, cos, sin, qg, kg, l, batch, seq):
    steps = seq // GQA_TQ
    qspec = pl.BlockSpec((GQA_TQ, GQA_WIDTH), lambda b, i: (b * steps + i, 0))
    kvspec = pl.BlockSpec((seq, KV_WIDTH), lambda b, i: (b, 0))
    return pl.pallas_call(
        _gqa_kernel,
        grid=(batch, steps),
        in_specs=[qspec, kvspec, kvspec, _const_spec((seq, LANES)), _const_spec((seq, LANES)),
                  _layer_spec(qg, l), _layer_spec(kg, l)],
        out_specs=qspec,
        out_shape=jax.ShapeDtypeStruct(q.shape, BF16),
        scratch_shapes=[pltpu.VMEM((seq, KV_WIDTH), BF16), pltpu.VMEM((KV_WIDTH + GQA_ONES_ROWS, seq), BF16)]
        + [pltpu.VMEM((GQA_KEY_CHUNK, GQA_Q_HEADS * GQA_UNIT), F32)] * (2 * GQA_CHAINS),
        compiler_params=_params("parallel", "arbitrary"),
        name="gqa_attention",
    )(q, k, v, cos, sin, qg, kg)


def _mem_kv_kernel(m_ref, g_ref, w_ref, o_ref):
    h = _rms(m_ref[...], g_ref[...]).astype(BF16)
    o_ref[...] = _dot(h, w_ref[...]).astype(BF16)


def _mem_kv(mem2, g, w, l):
    n = mem2.shape[0]
    width = w.shape[-1]
    return pl.pallas_call(
        _mem_kv_kernel,
        grid=(n // TM,),
        in_specs=[pl.BlockSpec((TM, D_MODEL), lambda i: (i, 0)), _layer_spec(g, l), _layer_spec(w, l)],
        out_specs=pl.BlockSpec((TM, width), lambda i: (i, 0)),
        out_shape=jax.ShapeDtypeStruct((n, width), BF16),
        compiler_params=_params("parallel"),
        name="mem_kv",
    )(mem2, g, w)


def _post_kernel(x_ref, na_ref, ga_ref, wo_ref, g_ref, wq_ref, kv_ref, wmo_ref, o_ref):
    x1 = x_ref[...] + _dot(na_ref[...], wo_ref[:NA_WIDTH, :]) + _dot(ga_ref[...], wo_ref[NA_WIDTH:, :])
    h = _rms(x1, g_ref[...]).astype(BF16)
    q = (_dot(h, wq_ref[...]) * (MEM_HEAD_DIM ** -0.5)).astype(BF16)
    heads = []
    for hh in range(MEM_HEADS):
        cols = slice(hh * MEM_HEAD_DIM, (hh + 1) * MEM_HEAD_DIM)
        s = _dot_nt(q[:, cols], kv_ref[:, cols])
        m = jnp.max(s, axis=-1, keepdims=True)
        p = jnp.exp(s - m)
        l = jnp.sum(p, axis=-1, keepdims=True)
        vcols = slice(MEM_WIDTH + hh * MEM_HEAD_DIM, MEM_WIDTH + (hh + 1) * MEM_HEAD_DIM)
        heads.append((_dot(p.astype(BF16), kv_ref[:, vcols]) / l).astype(BF16))
    o_ref[...] = x1 + _dot(jnp.concatenate(heads, axis=-1), wmo_ref[...])


def _post(x2, na, ga, wo, g, wq, kv, wmo, l, seq, mem_len):
    n = x2.shape[0]
    tiles_per_seq = seq // TM_PROJ
    row = lambda width: pl.BlockSpec((TM_PROJ, width), lambda i: (i, 0))
    return pl.pallas_call(
        _post_kernel,
        grid=(n // TM_PROJ,),
        in_specs=[row(D_MODEL), row(NA_WIDTH), row(GQA_WIDTH), _layer_spec(wo, l),
                  _layer_spec(g, l), _layer_spec(wq, l),
                  pl.BlockSpec((mem_len, 2 * MEM_WIDTH), lambda i: (i // tiles_per_seq, 0)),
                  _layer_spec(wmo, l)],
        out_specs=row(D_MODEL),
        out_shape=jax.ShapeDtypeStruct((n, D_MODEL), F32),
        compiler_params=_params("parallel"),
        name="post_mixer_memattn",
    )(x2, na, ga, wo, g, wq, kv, wmo)


def _ffn_kernel(xp_ref, x_ref, xn_ref, g_ref, wup_ref, cw_ref, cb_ref, wdn_ref, gf_ref, o_ref,
                *, tiles_per_seq, final_norm):
    i = pl.program_id(0)
    first = (i % tiles_per_seq) == 0
    last = (i % tiles_per_seq) == tiles_per_seq - 1
    x = x_ref[...]
    tm = x.shape[0]
    xa = jnp.concatenate([xp_ref[...], x, xn_ref[...]], axis=0)
    ha = _rms(xa, g_ref[...])
    n_all = tm + 2 * HALO
    rid = lax.broadcasted_iota(jnp.int32, (n_all, 1), 0)
    pad = (first & (rid < HALO)) | (last & (rid >= tm + HALO))
    ha = jnp.where(pad, 0.0, ha).astype(BF16)

    def conv(u, c0, width):
        cols = slice(c0, c0 + width)
        prev = pltpu.roll(u, 1, 0)[HALO:HALO + tm]
        nxt = pltpu.roll(u, n_all - 1, 0)[HALO:HALO + tm]
        return (prev * cw_ref[0:1, cols] + u[HALO:HALO + tm] * cw_ref[1:2, cols]
                + nxt * cw_ref[2:3, cols] + cb_ref[:, cols])

    def up(c):
        c0, width = FF_CHUNKS[c]
        return (_dot(ha, wup_ref[:, c0:c0 + width]), _dot(ha, wup_ref[:, D_FF + c0:D_FF + c0 + width]))

    acc = x
    u_next = up(0)
    for c, (c0, width) in enumerate(FF_CHUNKS):
        ug, uv = u_next
        if c + 1 < len(FF_CHUNKS):
            u_next = up(c + 1)
        gate = conv(ug, c0, width)
        val = conv(uv, D_FF + c0, width)
        act = (gate * jax.nn.sigmoid(gate) * val).astype(BF16)
        acc = acc + _dot(act, wdn_ref[c0:c0 + width, :])
    if final_norm:
        acc = _rms(acc, gf_ref[...])
    o_ref[...] = acc


def _ffn(x2, g, wup, cw, cb, wdn, gf, l, seq, final_norm):
    n = x2.shape[0]
    tiles_per_seq = seq // TM_PROJ
    hb = TM_PROJ // HALO
    nh = n // HALO
    return pl.pallas_call(
        functools.partial(_ffn_kernel, tiles_per_seq=tiles_per_seq, final_norm=final_norm),
        grid=(n // TM_PROJ,),
        in_specs=[pl.BlockSpec((HALO, D_MODEL), lambda i: (jnp.maximum(i * hb - 1, 0), 0)),
                  pl.BlockSpec((TM_PROJ, D_MODEL), lambda i: (i, 0)),
                  pl.BlockSpec((HALO, D_MODEL), lambda i: (jnp.minimum((i + 1) * hb, nh - 1), 0)),
                  _layer_spec(g, l), _layer_spec(wup, l), _layer_spec(cw, l),
                  _layer_spec(cb, l), _layer_spec(wdn, l), _const_spec((1, D_MODEL))],
        out_specs=pl.BlockSpec((TM_PROJ, D_MODEL), lambda i: (i, 0)),
        out_shape=jax.ShapeDtypeStruct((n, D_MODEL), F32),
        compiler_params=_params("parallel"),
        name="conv_ffn",
    )(x2, x2, x2, g, wup, cw, cb, wdn, gf)


def kernel(x, mem, norm_mix, w_in, na_rpb, gqa_q_norm, gqa_k_norm, w_out, norm_mem_q, norm_mem_kv,
           w_mem_q, w_mem_kv, w_mem_o, norm_ffn, w_up, conv_w, conv_b, w_down, norm_final):
    batch, seq, d = x.shape
    depth = w_in.shape[0]
    mem_len = mem.shape[1]
    rows = seq // GRID_W
    assert d == D_MODEL and seq % GRID_W == 0 and rows >= NA_KH and rows % NA_ROWS_PER_STEP == 0
    assert seq % TM == 0 and seq % TM_PROJ == 0 and seq % GQA_TQ == 0 and (batch * mem_len) % TM == 0

    cos, sin = _rope_tables(seq)
    x2 = x.reshape(batch * seq, d)
    mem2 = mem.reshape(batch * mem_len, d)
    rows_of = lambda a: a.reshape(a.shape[0], 1, -1)
    lanes2 = lambda a: jnp.tile(a, (1, LANES // HEAD_DIM)).reshape(a.shape[0], 1, LANES)
    norm_mix, norm_mem_q, norm_mem_kv, norm_ffn, conv_b = map(
        rows_of, (norm_mix, norm_mem_q, norm_mem_kv, norm_ffn, conv_b))
    q_gain, k_gain = lanes2(gqa_q_norm), lanes2(gqa_k_norm)
    w_in, w_out, w_mem_q, w_mem_kv, w_mem_o, w_up, w_down = (
        w.astype(BF16) for w in (w_in, w_out, w_mem_q, w_mem_kv, w_mem_o, w_up, w_down))
    rpb = na_rpb.reshape(depth, -1)

    for l in range(depth):
        naq, nak, nav, gq, gk, gv = _in_proj(x2, norm_mix, w_in, l)
        na = _na_attention(naq, nak, nav, _na_bias_table(rpb[l]), batch, seq)
        ga = _gqa_attention(gq, gk, gv, cos, sin, q_gain, k_gain, l, batch, seq)
        kv = _mem_kv(mem2, norm_mem_kv, w_mem_kv, l)
        x2 = _post(x2, na, ga, w_out, norm_mem_q, w_mem_q, kv, w_mem_o, l, seq, mem_len)
        x2 = _ffn(x2, norm_ffn, w_up, conv_w, conv_b, w_down, norm_final.reshape(1, -1), l, seq,
                  final_norm=(l == depth - 1))
    return x2.reshape(batch, seq, d)
```

```python
import functools

import jax
import jax.numpy as jnp
from jax import lax
from jax.experimental import pallas as pl
from jax.experimental.pallas import tpu as pltpu

D_MODEL = 1024
GRID_W = 64
HEAD_DIM = 64
NA_HEADS = 8
GQA_Q_HEADS = 8
GQA_KV_HEADS = 2
NA_KH = 8
NA_KW = 16
ROPE_THETA = 10000.0
NA_WIDTH = NA_HEADS * HEAD_DIM
GQA_WIDTH = GQA_Q_HEADS * HEAD_DIM
KV_WIDTH = GQA_KV_HEADS * HEAD_DIM
MEM_HEADS = 4
MEM_HEAD_DIM = 128
MEM_WIDTH = MEM_HEADS * MEM_HEAD_DIM
D_FF = 2816
EPS = 1e-6
NEG_INF = -1e30

LANES = 128
VMEM_LIMIT = 56 * 1024 * 1024

TM = 512
TM_PROJ = 1024
HALO = 8
MXU_TILE = 256
FF_CHUNKS = ((0, 6 * MXU_TILE), (6 * MXU_TILE, D_FF - 6 * MXU_TILE))
GQA_TQ = 4096
GQA_UNIT = 64
GQA_ONES_ROWS = 16
GQA_KEY_CHUNK = 512
GQA_KEY_TILE = 256
GQA_CHAINS = 2
GQA_SAFE_LOGIT = 100.0
GQA_BOUND_SLACK = 1.02
NA_ROWS_PER_STEP = 8
LOG2_E = 1.4426950408889634

BF16 = jnp.bfloat16
F32 = jnp.float32


def _rms(x, g):
    return x * lax.rsqrt(jnp.mean(x * x, axis=-1, keepdims=True) + EPS) * g


def _dot(a, b):
    return jnp.dot(a, b, preferred_element_type=F32)


def _dot_nt(a, b):
    return lax.dot_general(a, b, (((1,), (1,)), ((), ())), preferred_element_type=F32)


def _const_spec(shape):
    nd = len(shape)
    return pl.BlockSpec(shape, lambda *_: (0,) * nd, pipeline_mode=pl.Buffered(1))


def _layer_spec(stacked, l):
    tail = stacked.shape[1:]
    return pl.BlockSpec((None,) + tail, lambda *_: (l,) + (0,) * len(tail), pipeline_mode=pl.Buffered(1))


def _params(*sem):
    return pltpu.CompilerParams(dimension_semantics=sem, vmem_limit_bytes=VMEM_LIMIT)


def _cast_once(pairs):
    @pl.when(pl.program_id(0) == 0)
    def _():
        for src, dst in pairs:
            dst[...] = src[...].astype(BF16)


def _in_proj_kernel(x_ref, g_ref, w32_ref, naq_ref, nak_ref, nav_ref, gq_ref, gk_ref, gv_ref, w_ref):
    _cast_once([(w32_ref, w_ref)])
    h = _rms(x_ref[...], g_ref[...]).astype(BF16)
    o = 0
    naq_ref[...] = (_dot(h, w_ref[:, o:o + NA_WIDTH]) * (HEAD_DIM ** -0.5 * LOG2_E)).astype(BF16)
    o += NA_WIDTH
    nak_ref[...] = _dot(h, w_ref[:, o:o + NA_WIDTH]).astype(BF16)
    o += NA_WIDTH
    nav_ref[...] = _dot(h, w_ref[:, o:o + NA_WIDTH]).astype(BF16)
    o += NA_WIDTH
    gq_ref[...] = _dot(h, w_ref[:, o:o + GQA_WIDTH])
    o += GQA_WIDTH
    gkv = _dot(h, w_ref[:, o:o + 2 * KV_WIDTH])
    gk_ref[...] = gkv[:, :KV_WIDTH]
    gv_ref[...] = gkv[:, KV_WIDTH:].astype(BF16)


def _in_proj(x2, g, w, l):
    n = x2.shape[0]
    row = lambda width: pl.BlockSpec((TM_PROJ, width), lambda i: (i, 0))
    return pl.pallas_call(
        _in_proj_kernel,
        grid=(n // TM_PROJ,),
        in_specs=[row(D_MODEL), _layer_spec(g, l), _layer_spec(w, l)],
        out_specs=[row(NA_WIDTH), row(NA_WIDTH), row(NA_WIDTH), row(GQA_WIDTH), row(KV_WIDTH), row(KV_WIDTH)],
        out_shape=[
            jax.ShapeDtypeStruct((n, NA_WIDTH), BF16),
            jax.ShapeDtypeStruct((n, NA_WIDTH), BF16),
            jax.ShapeDtypeStruct((n, NA_WIDTH), BF16),
            jax.ShapeDtypeStruct((n, GQA_WIDTH), F32),
            jax.ShapeDtypeStruct((n, KV_WIDTH), F32),
            jax.ShapeDtypeStruct((n, KV_WIDTH), BF16),
        ],
        scratch_shapes=[pltpu.VMEM(w.shape[1:], BF16)],
        compiler_params=_params("arbitrary"),
        name="in_proj",
    )(x2, g, w)


N_ROW_OFF = 2 * NA_KH - 1
N_COL_OFF = 2 * NA_KW - 1


def _na_bias_kernel(rpb_ref, o_ref, t_ref):
    cls = pl.program_id(0)
    lane = lax.broadcasted_iota(jnp.int32, (GRID_W, LANES), 1)

    @pl.when(cls == 0)
    def _():
        q = lax.broadcasted_iota(jnp.int32, (GRID_W, LANES), 0)
        k = lane % GRID_W
        col_start = jnp.clip(q - NA_KW // 2, 0, GRID_W - NA_KW)
        valid = (k >= col_start) & (k < col_start + NA_KW)
        lane8 = lax.broadcasted_iota(jnp.int32, (8, LANES), 1)
        at = [((lane8 == (d - (NA_KW - 1)) % LANES) | (lane8 == (d - (NA_KW - 1) + GRID_W) % LANES))
              for d in range(N_COL_OFF)]

        def block(b, carry):
            base = jnp.zeros((8, LANES), F32)
            for d in range(N_COL_OFF):
                base = jnp.where(at[d], rpb_ref[b * N_COL_OFF + d], base)
            t = pltpu.roll(jnp.tile(base, (GRID_W // 8, 1)), 0, 1, stride=1, stride_axis=0)
            t_ref[b] = jnp.where(valid, t, NEG_INF) * LOG2_E
            return carry

        lax.fori_loop(0, NA_HEADS * N_ROW_OFF, block, 0)

    lo = lane < GRID_W
    for h in range(NA_HEADS):
        for ip in range(NA_KH // 2):
            ro = 2 * ip - cls + (NA_KH - 1)
            blk = jnp.where(lo, t_ref[h * N_ROW_OFF + ro], t_ref[h * N_ROW_OFF + ro + 1])
            o_ref[0, h // 2, (h % 2) * GRID_W:(h % 2 + 1) * GRID_W, ip * LANES:(ip + 1) * LANES] = blk


def _na_bias_table(rpb):
    shape = (NA_KH, NA_HEADS // 2, 2 * GRID_W, NA_KH * GRID_W)
    return pl.pallas_call(
        _na_bias_kernel,
        grid=(NA_KH,),
        in_specs=[pl.BlockSpec(memory_space=pltpu.SMEM)],
        out_specs=pl.BlockSpec((1,) + shape[1:], lambda c: (c, 0, 0, 0)),
        out_shape=jax.ShapeDtypeStruct(shape, F32),
        scratch_shapes=[pltpu.VMEM((NA_HEADS * N_ROW_OFF, GRID_W, LANES), F32)],
        compiler_params=_params("arbitrary"),
        name="na_bias_table",
    )(rpb.reshape(-1))


def _na_kernel(q_ref, k_ref, v_ref, bias_ref, o_ref, *, rows):
    rb = pl.program_id(1)
    lo = lax.broadcasted_iota(jnp.int32, (GRID_W, LANES), 1) < HEAD_DIM
    win = NA_KH * GRID_W
    pairs = [slice(j * LANES, (j + 1) * LANES) for j in range(NA_HEADS // 2)]
    ones = jnp.ones((win, LANES), BF16)

    def window(i):
        r = rb * NA_ROWS_PER_STEP + i
        row_start = jnp.clip(r - NA_KH // 2, 0, rows - NA_KH)
        return r - row_start, pl.multiple_of(row_start * GRID_W, GRID_W)

    def scores(i):
        cls, kstart = window(i)
        out = []
        for j, cols in enumerate(pairs):
            qp = q_ref[i * GRID_W:(i + 1) * GRID_W, cols]
            zero = jnp.zeros_like(qp)
            qs = jnp.concatenate([jnp.where(lo, qp, zero), jnp.where(lo, zero, qp)], axis=0)
            out.append(_dot_nt(qs, k_ref[pl.ds(kstart, win), cols]) + bias_ref[cls, j])
        return out

    def finish(i, s_list):
        _, kstart = window(i)
        p_list = [jnp.exp2(s - jnp.max(s, axis=-1, keepdims=True)).astype(BF16) for s in s_list]
        for p, cols in zip(p_list, pairs):
            o = _dot(p, jnp.concatenate([v_ref[pl.ds(kstart, win), cols], ones], axis=1))
            o = o[:, :LANES] / o[:, LANES:]
            o_ref[i * GRID_W:(i + 1) * GRID_W, cols] = jnp.where(lo, o[:GRID_W], o[GRID_W:]).astype(BF16)

    s_next = scores(0)
    for i in range(NA_ROWS_PER_STEP):
        s_cur = s_next
        if i + 1 < NA_ROWS_PER_STEP:
            s_next = scores(i + 1)
        finish(i, s_cur)


def _na_attention(q, k, v, bias, batch, seq):
    rows = seq // GRID_W
    tokens = NA_ROWS_PER_STEP * GRID_W
    steps = rows // NA_ROWS_PER_STEP
    qspec = pl.BlockSpec((tokens, NA_WIDTH), lambda b, r: (b * steps + r, 0))
    kvspec = pl.BlockSpec((seq, NA_WIDTH), lambda b, r: (b, 0))
    return pl.pallas_call(
        functools.partial(_na_kernel, rows=rows),
        grid=(batch, steps),
        in_specs=[qspec, kvspec, kvspec, _const_spec(bias.shape)],
        out_specs=qspec,
        out_shape=jax.ShapeDtypeStruct(q.shape, BF16),
        compiler_params=_params("parallel", "arbitrary"),
        name="na_attention",
    )(q, k, v, bias)


def _rope_tables(seq):
    t = jnp.arange(seq)
    pos = jnp.stack([t // GRID_W, t % GRID_W], axis=-1).astype(F32)
    n_f = HEAD_DIM // 4
    inv_freq = ROPE_THETA ** (-jnp.arange(n_f, dtype=F32) / n_f)
    ang = pos[:, :, None] * inv_freq
    cos, sin = jnp.cos(ang), jnp.sin(ang)
    cos_h = jnp.concatenate([cos, cos], axis=-1).reshape(seq, HEAD_DIM)
    sin_h = jnp.concatenate([-sin, sin], axis=-1).reshape(seq, HEAD_DIM)
    return jnp.tile(cos_h, (1, LANES // HEAD_DIM)), jnp.tile(sin_h, (1, LANES // HEAD_DIM))


def _norm_rope(x, gain, cos, sin, lo, first_half):
    x2 = x * x
    zero = jnp.zeros_like(x2)
    ms_lo = jnp.sum(jnp.where(lo, x2, zero), axis=-1, keepdims=True) * (1.0 / HEAD_DIM)
    ms_hi = jnp.sum(jnp.where(lo, zero, x2), axis=-1, keepdims=True) * (1.0 / HEAD_DIM)
    r = jnp.where(lo, lax.rsqrt(ms_lo + EPS), lax.rsqrt(ms_hi + EPS))
    y = x * r * gain
    n_f = HEAD_DIM // 4
    partner = jnp.where(first_half, pltpu.roll(y, LANES - n_f, 1), pltpu.roll(y, n_f, 1))
    return y * cos + partner * sin


def _gqa_kernel(q_ref, k_ref, v_ref, cos_ref, sin_ref, qg_ref, kg_ref, o_ref, kn_ref, vx_ref, *s_refs):
    qi = pl.program_id(1)
    seq = k_ref.shape[0]
    n_units = GQA_TQ // GQA_UNIT
    n_chunks = seq // GQA_KEY_CHUNK
    heads_per_group = GQA_Q_HEADS // GQA_KV_HEADS
    stacked_rows = GQA_Q_HEADS * GQA_UNIT

    @pl.when(qi == 0)
    def _():
        lane_k = lax.broadcasted_iota(jnp.int32, (seq, LANES), 1)
        kn = _norm_rope(k_ref[...], kg_ref[...], cos_ref[...], sin_ref[...],
                        lane_k < HEAD_DIM, (lane_k % (HEAD_DIM // 2)) < HEAD_DIM // 4)
        kn_ref[...] = kn.astype(BF16)
        vx_ref[:KV_WIDTH, :] = v_ref[...].astype(F32).T.astype(BF16)
        vx_ref[KV_WIDTH:, :] = jnp.ones((GQA_ONES_ROWS, seq), BF16)

    lane = lax.broadcasted_iota(jnp.int32, (GQA_UNIT, LANES), 1)
    lo = lane < HEAD_DIM
    first_half = (lane % (HEAD_DIM // 2)) < HEAD_DIM // 4
    zero = jnp.zeros((GQA_UNIT, LANES), F32)
    scale = HEAD_DIM ** -0.5 * LOG2_E

    def prep_q(u):
        r0 = pl.multiple_of(u * GQA_UNIT, GQA_UNIT)
        t0 = pl.multiple_of(qi * GQA_TQ + u * GQA_UNIT, GQA_UNIT)
        cos = cos_ref[pl.ds(t0, GQA_UNIT), :]
        sin = sin_ref[pl.ds(t0, GQA_UNIT), :]
        stacked = []
        for pair in range(GQA_Q_HEADS // 2):
            qp = q_ref[pl.ds(r0, GQA_UNIT), pair * LANES:(pair + 1) * LANES]
            qn = _norm_rope(qp, qg_ref[...], cos, sin, lo, first_half) * scale
            sw = pltpu.roll(qn, HEAD_DIM, 1)
            if pair < heads_per_group // 2:
                stacked += [jnp.where(lo, qn, zero), jnp.where(lo, sw, zero)]
            else:
                stacked += [jnp.where(lo, zero, sw), jnp.where(lo, zero, qn)]
        return jnp.concatenate(stacked, axis=0).T.astype(BF16)

    def s_scr(k, c):
        return s_refs[2 * k + c % 2]

    def chunk_scores(c, qt, k, with_max=True):
        s = _dot(kn_ref[c * GQA_KEY_CHUNK:(c + 1) * GQA_KEY_CHUNK, :], qt)
        s_scr(k, c)[...] = s
        return jnp.max(s, axis=0, keepdims=True) if with_max else None

    def write_out(u, acc):
        r0 = pl.multiple_of(u * GQA_UNIT, GQA_UNIT)
        o = (acc[:KV_WIDTH] / acc[KV_WIDTH:KV_WIDTH + 1]).T
        for pair in range(GQA_Q_HEADS // 2):
            even = o[(2 * pair) * GQA_UNIT:(2 * pair + 1) * GQA_UNIT]
            odd = o[(2 * pair + 1) * GQA_UNIT:(2 * pair + 2) * GQA_UNIT]
            if pair < heads_per_group // 2:
                out = jnp.where(lo, even, pltpu.roll(odd, HEAD_DIM, 1))
            else:
                out = jnp.where(lo, pltpu.roll(even, HEAD_DIM, 1), odd)
            o_ref[pl.ds(r0, GQA_UNIT), pair * LANES:(pair + 1) * LANES] = out.astype(BF16)

    chains = range(GQA_CHAINS)

    def trip(i, carry):
        qt, cmax, acc_prev = [list(x) for x in carry]
        qt_next = [None] * GQA_CHAINS
        units = [i * GQA_CHAINS + k for k in chains]
        m = [jnp.full((1, stacked_rows), NEG_INF, F32) for _ in chains]
        acc = [jnp.zeros((KV_WIDTH + GQA_ONES_ROWS, stacked_rows), F32) for _ in chains]
        for c in range(n_chunks):
            for k in chains:
                m_new = jnp.maximum(m[k], cmax[k])
                if c + 1 < n_chunks:
                    cmax[k] = chunk_scores(c + 1, qt[k], k)
                else:
                    cmax[k] = chunk_scores(0, qt_next[k], k)
                if c == 0:
                    write_out(jnp.maximum(units[k] - GQA_CHAINS, k), acc_prev[k])
                if c == 1:
                    qt_next[k] = prep_q(jnp.minimum(units[k] + GQA_CHAINS, n_units - GQA_CHAINS + k))
                pv = None
                for t in range(GQA_KEY_CHUNK // GQA_KEY_TILE):
                    rows = slice(t * GQA_KEY_TILE, (t + 1) * GQA_KEY_TILE)
                    k0 = c * GQA_KEY_CHUNK + t * GQA_KEY_TILE
                    p = jnp.exp2(s_scr(k, c)[rows, :] - m_new).astype(BF16)
                    d = _dot(vx_ref[:, k0:k0 + GQA_KEY_TILE], p)
                    pv = d if pv is None else pv + d
                acc[k] = acc[k] * jnp.exp2(m[k] - m_new) + pv
                m[k] = m_new
        return tuple(qt_next), tuple(cmax), tuple(acc)

    def trip_bounded(i, carry):
        qt, acc_prev = [list(x) for x in carry]
        qt_next = [None] * GQA_CHAINS
        units = [i * GQA_CHAINS + k for k in chains]
        acc = [None] * GQA_CHAINS
        for c in range(n_chunks):
            rows = slice(c * GQA_KEY_CHUNK, (c + 1) * GQA_KEY_CHUNK)
            for k in chains:
                if c + 1 < n_chunks:
                    chunk_scores(c + 1, qt[k], k, with_max=False)
                else:
                    chunk_scores(0, qt_next[k], k, with_max=False)
                if c == 0:
                    write_out(jnp.maximum(units[k] - GQA_CHAINS, k), acc_prev[k])
                if c == 1:
                    qt_next[k] = prep_q(jnp.minimum(units[k] + GQA_CHAINS, n_units - GQA_CHAINS + k))
                d = _dot(vx_ref[:, rows], jnp.exp2(s_scr(k, c)[...]).astype(BF16))
                acc[k] = d if acc[k] is None else acc[k] + d
        return tuple(qt_next), tuple(acc)

    placeholder = tuple(jnp.ones((KV_WIDTH + GQA_ONES_ROWS, stacked_rows), F32) for _ in chains)

    def run_exact():
        assert n_chunks % 2 == 0
        qt0 = tuple(prep_q(k) for k in chains)
        init = (qt0, tuple(chunk_scores(0, qt0[k], k) for k in chains), placeholder)
        _, _, acc_last = lax.fori_loop(0, n_units // GQA_CHAINS, trip, init)
        for k in chains:
            write_out(n_units - GQA_CHAINS + k, acc_last[k])

    def run_bounded():
        qt0 = tuple(prep_q(k) for k in chains)
        for k in chains:
            chunk_scores(0, qt0[k], k, with_max=False)
        init = (qt0, placeholder)
        _, acc_last = lax.fori_loop(0, n_units // GQA_CHAINS, trip_bounded, init)
        for k in chains:
            write_out(n_units - GQA_CHAINS + k, acc_last[k])

    logit_bound = (HEAD_DIM * scale * GQA_BOUND_SLACK) * jnp.max(jnp.abs(qg_ref[...])) * jnp.max(jnp.abs(kg_ref[...]))
    lax.cond(logit_bound <= GQA_SAFE_LOGIT, run_bounded, run_exact)


def _gqa_attention(q, k, v, cos, sin, qg, kg, l, batch, seq):
    steps = seq // GQA_TQ
    qspec = pl.BlockSpec((GQA_TQ, GQA_WIDTH), lambda b, i: (b * steps + i, 0))
    kvspec = pl.BlockSpec((seq, KV_WIDTH), lambda b, i: (b, 0))
    return pl.pallas_call(
        _gqa_kernel,
        grid=(batch, steps),
        in_specs=[qspec, kvspec, kvspec, _const_spec((seq, LANES)), _const_spec((seq, LANES)),
                  _layer_spec(qg, l), _layer_spec(kg, l)],
        out_specs=qspec,
        out_shape=jax.ShapeDtypeStruct(q.shape, BF16),
        scratch_shapes=[pltpu.VMEM((seq, KV_WIDTH), BF16), pltpu.VMEM((KV_WIDTH + GQA_ONES_ROWS, seq), BF16)]
        + [pltpu.VMEM((GQA_KEY_CHUNK, GQA_Q_HEADS * GQA_UNIT), F32)] * (2 * GQA_CHAINS),
        compiler_params=_params("parallel", "arbitrary"),
        name="gqa_attention",
    )(q, k, v, cos, sin, qg, kg)


def _mem_kv_kernel(m_ref, g_ref, w32_ref, o_ref, w_ref):
    _cast_once([(w32_ref, w_ref)])
    h = _rms(m_ref[...], g_ref[...]).astype(BF16)
    o_ref[...] = _dot(h, w_ref[...]).astype(BF16)


def _mem_kv(mem2, g, w, l):
    n = mem2.shape[0]
    width = w.shape[-1]
    return pl.pallas_call(
        _mem_kv_kernel,
        grid=(n // TM,),
        in_specs=[pl.BlockSpec((TM, D_MODEL), lambda i: (i, 0)), _layer_spec(g, l), _layer_spec(w, l)],
        out_specs=pl.BlockSpec((TM, width), lambda i: (i, 0)),
        out_shape=jax.ShapeDtypeStruct((n, width), BF16),
        scratch_shapes=[pltpu.VMEM(w.shape[1:], BF16)],
        compiler_params=_params("arbitrary"),
        name="mem_kv",
    )(mem2, g, w)


def _post_kernel(x_ref, na_ref, ga_ref, wo32_ref, g_ref, wq32_ref, kv_ref, wmo32_ref, o_ref,
                 wo_ref, wq_ref, wmo_ref):
    _cast_once([(wo32_ref, wo_ref), (wq32_ref, wq_ref), (wmo32_ref, wmo_ref)])
    x1 = x_ref[...] + _dot(na_ref[...], wo_ref[:NA_WIDTH, :]) + _dot(ga_ref[...], wo_ref[NA_WIDTH:, :])
    h = _rms(x1, g_ref[...]).astype(BF16)
    q = (_dot(h, wq_ref[...]) * (MEM_HEAD_DIM ** -0.5)).astype(BF16)
    heads = []
    for hh in range(MEM_HEADS):
        cols = slice(hh * MEM_HEAD_DIM, (hh + 1) * MEM_HEAD_DIM)
        s = _dot_nt(q[:, cols], kv_ref[:, cols])
        m = jnp.max(s, axis=-1, keepdims=True)
        p = jnp.exp(s - m)
        l = jnp.sum(p, axis=-1, keepdims=True)
        vcols = slice(MEM_WIDTH + hh * MEM_HEAD_DIM, MEM_WIDTH + (hh + 1) * MEM_HEAD_DIM)
        heads.append((_dot(p.astype(BF16), kv_ref[:, vcols]) / l).astype(BF16))
    o_ref[...] = x1 + _dot(jnp.concatenate(heads, axis=-1), wmo_ref[...])


def _post(x2, na, ga, wo, g, wq, kv, wmo, l, seq, mem_len):
    n = x2.shape[0]
    tiles_per_seq = seq // TM_PROJ
    row = lambda width: pl.BlockSpec((TM_PROJ, width), lambda i: (i, 0))
    return pl.pallas_call(
        _post_kernel,
        grid=(n // TM_PROJ,),
        in_specs=[row(D_MODEL), row(NA_WIDTH), row(GQA_WIDTH), _layer_spec(wo, l),
                  _layer_spec(g, l), _layer_spec(wq, l),
                  pl.BlockSpec((mem_len, 2 * MEM_WIDTH), lambda i: (i // tiles_per_seq, 0)),
                  _layer_spec(wmo, l)],
        out_specs=row(D_MODEL),
        out_shape=jax.ShapeDtypeStruct((n, D_MODEL), F32),
        scratch_shapes=[pltpu.VMEM(w.shape[1:], BF16) for w in (wo, wq, wmo)],
        compiler_params=_params("arbitrary"),
        name="post_mixer_memattn",
    )(x2, na, ga, wo, g, wq, kv, wmo)


def _ffn_kernel(xp_ref, x_ref, xn_ref, g_ref, wup_ref, cw_ref, cb_ref, wdn_ref, gf_ref, o_ref,
                *, tiles_per_seq, final_norm):
    i = pl.program_id(0)
    first = (i % tiles_per_seq) == 0
    last = (i % tiles_per_seq) == tiles_per_seq - 1
    x = x_ref[...]
    tm = x.shape[0]
    xa = jnp.concatenate([xp_ref[...], x, xn_ref[...]], axis=0)
    ha = _rms(xa, g_ref[...])
    n_all = tm + 2 * HALO
    rid = lax.broadcasted_iota(jnp.int32, (n_all, 1), 0)
    pad = (first & (rid < HALO)) | (last & (rid >= tm + HALO))
    ha = jnp.where(pad, 0.0, ha).astype(BF16)

    def conv(u, c0, width):
        cols = slice(c0, c0 + width)
        prev = pltpu.roll(u, 1, 0)[HALO:HALO + tm]
        nxt = pltpu.roll(u, n_all - 1, 0)[HALO:HALO + tm]
        return (prev * cw_ref[0:1, cols] + u[HALO:HALO + tm] * cw_ref[1:2, cols]
                + nxt * cw_ref[2:3, cols] + cb_ref[:, cols])

    def up(c):
        c0, width = FF_CHUNKS[c]
        return (_dot(ha, wup_ref[:, c0:c0 + width]), _dot(ha, wup_ref[:, D_FF + c0:D_FF + c0 + width]))

    acc = x
    u_next = up(0)
    for c, (c0, width) in enumerate(FF_CHUNKS):
        ug, uv = u_next
        if c + 1 < len(FF_CHUNKS):
            u_next = up(c + 1)
        gate = conv(ug, c0, width)
        val = conv(uv, D_FF + c0, width)
        act = (gate * jax.nn.sigmoid(gate) * val).astype(BF16)
        acc = acc + _dot(act, wdn_ref[c0:c0 + width, :])
    if final_norm:
        acc = _rms(acc, gf_ref[...])
    o_ref[...] = acc


def _ffn(x2, g, wup, cw, cb, wdn, gf, l, seq, final_norm):
    n = x2.shape[0]
    tiles_per_seq = seq // TM_PROJ
    hb = TM_PROJ // HALO
    nh = n // HALO
    return pl.pallas_call(
        functools.partial(_ffn_kernel, tiles_per_seq=tiles_per_seq, final_norm=final_norm),
        grid=(n // TM_PROJ,),
        in_specs=[pl.BlockSpec((HALO, D_MODEL), lambda i: (jnp.maximum(i * hb - 1, 0), 0)),
                  pl.BlockSpec((TM_PROJ, D_MODEL), lambda i: (i, 0)),
                  pl.BlockSpec((HALO, D_MODEL), lambda i: (jnp.minimum((i + 1) * hb, nh - 1), 0)),
                  _layer_spec(g, l), _layer_spec(wup, l), _layer_spec(cw, l),
                  _layer_spec(cb, l), _layer_spec(wdn, l), _const_spec((1, D_MODEL))],
        out_specs=pl.BlockSpec((TM_PROJ, D_MODEL), lambda i: (i, 0)),
        out_shape=jax.ShapeDtypeStruct((n, D_MODEL), F32),
        compiler_params=_params("parallel"),
        name="conv_ffn",
    )(x2, x2, x2, g, wup, cw, cb, wdn, gf)


def kernel(x, mem, norm_mix, w_in, na_rpb, gqa_q_norm, gqa_k_norm, w_out, norm_mem_q, norm_mem_kv,
           w_mem_q, w_mem_kv, w_mem_o, norm_ffn, w_up, conv_w, conv_b, w_down, norm_final):
    batch, seq, d = x.shape
    depth = w_in.shape[0]
    mem_len = mem.shape[1]
    rows = seq // GRID_W
    assert d == D_MODEL and seq % GRID_W == 0 and rows >= NA_KH and rows % NA_ROWS_PER_STEP == 0
    assert seq % TM == 0 and seq % TM_PROJ == 0 and seq % GQA_TQ == 0 and (batch * mem_len) % TM == 0

    cos, sin = _rope_tables(seq)
    x2 = x.reshape(batch * seq, d)
    mem2 = mem.reshape(batch * mem_len, d)
    rows_of = lambda a: a.reshape(a.shape[0], 1, -1)
    lanes2 = lambda a: jnp.tile(a, (1, LANES // HEAD_DIM)).reshape(a.shape[0], 1, LANES)
    norm_mix, norm_mem_q, norm_mem_kv, norm_ffn, conv_b = map(
        rows_of, (norm_mix, norm_mem_q, norm_mem_kv, norm_ffn, conv_b))
    q_gain, k_gain = lanes2(gqa_q_norm), lanes2(gqa_k_norm)
    w_up, w_down = w_up.astype(BF16), w_down.astype(BF16)
    rpb = na_rpb.reshape(depth, -1)

    for l in range(depth):
        naq, nak, nav, gq, gk, gv = _in_proj(x2, norm_mix, w_in, l)
        na = _na_attention(naq, nak, nav, _na_bias_table(rpb[l]), batch, seq)
        ga = _gqa_attention(gq, gk, gv, cos, sin, q_gain, k_gain, l, batch, seq)
        kv = _mem_kv(mem2, norm_mem_kv, w_mem_kv, l)
        x2 = _post(x2, na, ga, w_out, norm_mem_q, w_mem_q, kv, w_mem_o, l, seq, mem_len)
        x2 = _ffn(x2, norm_ffn, w_up, conv_w, conv_b, w_down, norm_final.reshape(1, -1), l, seq,
                  final_norm=(l == depth - 1))
    return x2.reshape(batch, seq, d)
```

```python
import functools

import jax
import jax.numpy as jnp
from jax import lax
from jax.experimental import pallas as pl
from jax.experimental.pallas import tpu as pltpu

D_MODEL = 1024
GRID_W = 64
HEAD_DIM = 64
NA_HEADS = 8
GQA_Q_HEADS = 8
GQA_KV_HEADS = 2
NA_KH = 8
NA_KW = 16
ROPE_THETA = 10000.0
NA_WIDTH = NA_HEADS * HEAD_DIM
GQA_WIDTH = GQA_Q_HEADS * HEAD_DIM
KV_WIDTH = GQA_KV_HEADS * HEAD_DIM
MEM_HEADS = 4
MEM_HEAD_DIM = 128
MEM_WIDTH = MEM_HEADS * MEM_HEAD_DIM
D_FF = 2816
EPS = 1e-6
NEG_INF = -1e30

LANES = 128
SUBLANES = 8
VMEM_LIMIT = 56 * 1024 * 1024

TM = 512
TM_PROJ = 1024
HALO = SUBLANES
MXU_TILE = 256
FF_CHUNKS = ((0, 6 * MXU_TILE), (6 * MXU_TILE, D_FF - 6 * MXU_TILE))
GQA_TQ = 4096
GQA_UNIT = 64
GQA_ONES_ROWS = 16
GQA_KEY_CHUNK = 512
GQA_KEY_TILE = 256
GQA_CHAINS = 2
GQA_SAFE_LOGIT = 100.0
GQA_BOUND_SLACK = 1.02
NA_ROWS_PER_STEP = 16
LOG2_E = 1.4426950408889634

BF16 = jnp.bfloat16
F32 = jnp.float32


def _rms(x, g):
    return x * lax.rsqrt(jnp.mean(x * x, axis=-1, keepdims=True) + EPS) * g


def _dot(a, b):
    return jnp.dot(a, b, preferred_element_type=F32)


def _dot_nt(a, b):
    return lax.dot_general(a, b, (((1,), (1,)), ((), ())), preferred_element_type=F32)


def _const_spec(shape):
    nd = len(shape)
    return pl.BlockSpec(shape, lambda *_: (0,) * nd, pipeline_mode=pl.Buffered(1))


def _layer_spec(stacked, l):
    tail = stacked.shape[1:]
    return pl.BlockSpec((None,) + tail, lambda *_: (l,) + (0,) * len(tail), pipeline_mode=pl.Buffered(1))


def _params(*sem):
    return pltpu.CompilerParams(dimension_semantics=sem, vmem_limit_bytes=VMEM_LIMIT)


def _cast_once(pairs):
    @pl.when(pl.program_id(0) == 0)
    def _():
        for src, dst in pairs:
            dst[...] = src[...].astype(BF16)


def _in_proj_kernel(x_ref, g_ref, w32_ref, naq_ref, nak_ref, nav_ref, gq_ref, gk_ref, gv_ref, w_ref):
    _cast_once([(w32_ref, w_ref)])
    h = _rms(x_ref[...], g_ref[...]).astype(BF16)
    o = 0
    naq_ref[...] = (_dot(h, w_ref[:, o:o + NA_WIDTH]) * (HEAD_DIM ** -0.5 * LOG2_E)).astype(BF16)
    o += NA_WIDTH
    nak_ref[...] = _dot(h, w_ref[:, o:o + NA_WIDTH]).astype(BF16)
    o += NA_WIDTH
    nav_ref[...] = _dot(h, w_ref[:, o:o + NA_WIDTH]).astype(BF16)
    o += NA_WIDTH
    gq_ref[...] = _dot(h, w_ref[:, o:o + GQA_WIDTH])
    o += GQA_WIDTH
    gkv = _dot(h, w_ref[:, o:o + 2 * KV_WIDTH])
    gk_ref[...] = gkv[:, :KV_WIDTH]
    gv_ref[...] = gkv[:, KV_WIDTH:].astype(BF16)


def _in_proj(x2, g, w, l):
    n = x2.shape[0]
    row = lambda width: pl.BlockSpec((TM_PROJ, width), lambda i: (i, 0))
    return pl.pallas_call(
        _in_proj_kernel,
        grid=(n // TM_PROJ,),
        in_specs=[row(D_MODEL), _layer_spec(g, l), _layer_spec(w, l)],
        out_specs=[row(NA_WIDTH), row(NA_WIDTH), row(NA_WIDTH), row(GQA_WIDTH), row(KV_WIDTH), row(KV_WIDTH)],
        out_shape=[
            jax.ShapeDtypeStruct((n, NA_WIDTH), BF16),
            jax.ShapeDtypeStruct((n, NA_WIDTH), BF16),
            jax.ShapeDtypeStruct((n, NA_WIDTH), BF16),
            jax.ShapeDtypeStruct((n, GQA_WIDTH), F32),
            jax.ShapeDtypeStruct((n, KV_WIDTH), F32),
            jax.ShapeDtypeStruct((n, KV_WIDTH), BF16),
        ],
        scratch_shapes=[pltpu.VMEM(w.shape[1:], BF16)],
        compiler_params=_params("arbitrary"),
        name="in_proj",
    )(x2, g, w)


N_ROW_OFF = 2 * NA_KH - 1
N_COL_OFF = 2 * NA_KW - 1


def _build_na_bias(rpb_ref, t_ref, bias_ref):
    lane = lax.broadcasted_iota(jnp.int32, (GRID_W, LANES), 1)
    q = lax.broadcasted_iota(jnp.int32, (GRID_W, LANES), 0)
    k = lane % GRID_W
    col_start = jnp.clip(q - NA_KW // 2, 0, GRID_W - NA_KW)
    valid = (k >= col_start) & (k < col_start + NA_KW)
    lane8 = lax.broadcasted_iota(jnp.int32, (SUBLANES, LANES), 1)
    at = [((lane8 == (d - (NA_KW - 1)) % LANES) | (lane8 == (d - (NA_KW - 1) + GRID_W) % LANES))
          for d in range(N_COL_OFF)]

    def block(b, carry):
        base = jnp.zeros((SUBLANES, LANES), F32)
        for d in range(N_COL_OFF):
            base = jnp.where(at[d], rpb_ref[b * N_COL_OFF + d], base)
        t = pltpu.roll(jnp.tile(base, (GRID_W // SUBLANES, 1)), 0, 1, stride=1, stride_axis=0)
        t_ref[b] = jnp.where(valid, t, NEG_INF) * LOG2_E
        return carry

    lax.fori_loop(0, NA_HEADS * N_ROW_OFF, block, 0, unroll=NA_HEADS)

    lo = lane < GRID_W

    def assemble(cls, carry):
        for h in range(NA_HEADS):
            for ip in range(NA_KH // 2):
                ro = 2 * ip - cls + (NA_KH - 1)
                blk = jnp.where(lo, t_ref[h * N_ROW_OFF + ro], t_ref[h * N_ROW_OFF + ro + 1])
                bias_ref[cls, h // 2, (h % 2) * GRID_W:(h % 2 + 1) * GRID_W, ip * LANES:(ip + 1) * LANES] = blk
        return carry

    lax.fori_loop(0, NA_KH, assemble, 0)


def _na_kernel(rpb_ref, q_ref, k_ref, v_ref, o_ref, t_ref, bias_ref, *, rows):
    rb = pl.program_id(1)

    @pl.when((pl.program_id(0) == 0) & (rb == 0))
    def _():
        _build_na_bias(rpb_ref, t_ref, bias_ref)

    lo = lax.broadcasted_iota(jnp.int32, (GRID_W, LANES), 1) < HEAD_DIM
    win = NA_KH * GRID_W
    pairs = [slice(j * LANES, (j + 1) * LANES) for j in range(NA_HEADS // 2)]
    ones = jnp.ones((win, LANES), BF16)

    def window(i):
        r = rb * NA_ROWS_PER_STEP + i
        row_start = jnp.clip(r - NA_KH // 2, 0, rows - NA_KH)
        return r - row_start, pl.multiple_of(row_start * GRID_W, GRID_W)

    def scores(i):
        cls, kstart = window(i)
        out = []
        for j, cols in enumerate(pairs):
            qp = q_ref[i * GRID_W:(i + 1) * GRID_W, cols]
            zero = jnp.zeros_like(qp)
            qs = jnp.concatenate([jnp.where(lo, qp, zero), jnp.where(lo, zero, qp)], axis=0)
            out.append(_dot_nt(qs, k_ref[pl.ds(kstart, win), cols]) + bias_ref[cls, j])
        return out

    def finish(i, s_list):
        _, kstart = window(i)
        p_list = [jnp.exp2(s - jnp.max(s, axis=-1, keepdims=True)).astype(BF16) for s in s_list]
        for p, cols in zip(p_list, pairs):
            o = _dot(p, jnp.concatenate([v_ref[pl.ds(kstart, win), cols], ones], axis=1))
            o = o[:, :LANES] / o[:, LANES:]
            o_ref[i * GRID_W:(i + 1) * GRID_W, cols] = jnp.where(lo, o[:GRID_W], o[GRID_W:]).astype(BF16)

    s_next = scores(0)
    for i in range(NA_ROWS_PER_STEP):
        s_cur = s_next
        if i + 1 < NA_ROWS_PER_STEP:
            s_next = scores(i + 1)
        finish(i, s_cur)


def _na_attention(rpb, q, k, v, batch, seq):
    rows = seq // GRID_W
    tokens = NA_ROWS_PER_STEP * GRID_W
    steps = rows // NA_ROWS_PER_STEP
    qspec = pl.BlockSpec((tokens, NA_WIDTH), lambda b, r: (b * steps + r, 0))
    kvspec = pl.BlockSpec((seq, NA_WIDTH), lambda b, r: (b, 0))
    return pl.pallas_call(
        functools.partial(_na_kernel, rows=rows),
        grid=(batch, steps),
        in_specs=[pl.BlockSpec(memory_space=pltpu.SMEM), qspec, kvspec, kvspec],
        out_specs=qspec,
        out_shape=jax.ShapeDtypeStruct(q.shape, BF16),
        scratch_shapes=[pltpu.VMEM((NA_HEADS * N_ROW_OFF, GRID_W, LANES), F32),
                        pltpu.VMEM((NA_KH, NA_HEADS // 2, 2 * GRID_W, NA_KH * GRID_W), F32)],
        compiler_params=_params("arbitrary", "arbitrary"),
        name="na_attention",
    )(rpb, q, k, v)


def _rope_tables(seq):
    t = jnp.arange(seq)
    pos = jnp.stack([t // GRID_W, t % GRID_W], axis=-1).astype(F32)
    n_f = HEAD_DIM // 4
    inv_freq = ROPE_THETA ** (-jnp.arange(n_f, dtype=F32) / n_f)
    ang = pos[:, :, None] * inv_freq
    cos, sin = jnp.cos(ang), jnp.sin(ang)
    cos_h = jnp.concatenate([cos, cos], axis=-1).reshape(seq, HEAD_DIM)
    sin_h = jnp.concatenate([-sin, sin], axis=-1).reshape(seq, HEAD_DIM)
    return jnp.tile(cos_h, (1, LANES // HEAD_DIM)), jnp.tile(sin_h, (1, LANES // HEAD_DIM))


def _norm_rope(x, gain, cos, sin, lo, first_half):
    x2 = x * x
    zero = jnp.zeros_like(x2)
    ms_lo = jnp.sum(jnp.where(lo, x2, zero), axis=-1, keepdims=True) * (1.0 / HEAD_DIM)
    ms_hi = jnp.sum(jnp.where(lo, zero, x2), axis=-1, keepdims=True) * (1.0 / HEAD_DIM)
    r = jnp.where(lo, lax.rsqrt(ms_lo + EPS), lax.rsqrt(ms_hi + EPS))
    y = x * r * gain
    n_f = HEAD_DIM // 4
    partner = jnp.where(first_half, pltpu.roll(y, LANES - n_f, 1), pltpu.roll(y, n_f, 1))
    return y * cos + partner * sin


def _gqa_kernel(q_ref, k_ref, v_ref, cos_ref, sin_ref, qg_ref, kg_ref, o_ref, kn_ref, vx_ref, *s_refs):
    qi = pl.program_id(1)
    seq = k_ref.shape[0]
    n_units = GQA_TQ // GQA_UNIT
    n_chunks = seq // GQA_KEY_CHUNK
    heads_per_group = GQA_Q_HEADS // GQA_KV_HEADS
    stacked_rows = GQA_Q_HEADS * GQA_UNIT

    @pl.when(qi == 0)
    def _():
        lane_k = lax.broadcasted_iota(jnp.int32, (seq, LANES), 1)
        kn = _norm_rope(k_ref[...], kg_ref[...], cos_ref[...], sin_ref[...],
                        lane_k < HEAD_DIM, (lane_k % (HEAD_DIM // 2)) < HEAD_DIM // 4)
        kn_ref[...] = kn.astype(BF16)
        vx_ref[:KV_WIDTH, :] = v_ref[...].astype(F32).T.astype(BF16)
        vx_ref[KV_WIDTH:, :] = jnp.ones((GQA_ONES_ROWS, seq), BF16)

    lane = lax.broadcasted_iota(jnp.int32, (GQA_UNIT, LANES), 1)
    lo = lane < HEAD_DIM
    first_half = (lane % (HEAD_DIM // 2)) < HEAD_DIM // 4
    zero = jnp.zeros((GQA_UNIT, LANES), F32)
    scale = HEAD_DIM ** -0.5 * LOG2_E

    def prep_q(u):
        r0 = pl.multiple_of(u * GQA_UNIT, GQA_UNIT)
        t0 = pl.multiple_of(qi * GQA_TQ + u * GQA_UNIT, GQA_UNIT)
        cos = cos_ref[pl.ds(t0, GQA_UNIT), :]
        sin = sin_ref[pl.ds(t0, GQA_UNIT), :]
        stacked = []
        for pair in range(GQA_Q_HEADS // 2):
            qp = q_ref[pl.ds(r0, GQA_UNIT), pair * LANES:(pair + 1) * LANES]
            qn = _norm_rope(qp, qg_ref[...], cos, sin, lo, first_half) * scale
            sw = pltpu.roll(qn, HEAD_DIM, 1)
            if pair < heads_per_group // 2:
                stacked += [jnp.where(lo, qn, zero), jnp.where(lo, sw, zero)]
            else:
                stacked += [jnp.where(lo, zero, sw), jnp.where(lo, zero, qn)]
        return jnp.concatenate(stacked, axis=0).T.astype(BF16)

    def s_scr(k, c):
        return s_refs[2 * k + c % 2]

    def chunk_scores(c, qt, k, with_max=True):
        s = _dot(kn_ref[c * GQA_KEY_CHUNK:(c + 1) * GQA_KEY_CHUNK, :], qt)
        s_scr(k, c)[...] = s
        return jnp.max(s, axis=0, keepdims=True) if with_max else None

    def write_out(u, acc):
        r0 = pl.multiple_of(u * GQA_UNIT, GQA_UNIT)
        o = (acc[:KV_WIDTH] / acc[KV_WIDTH:KV_WIDTH + 1]).T
        for pair in range(GQA_Q_HEADS // 2):
            even = o[(2 * pair) * GQA_UNIT:(2 * pair + 1) * GQA_UNIT]
            odd = o[(2 * pair + 1) * GQA_UNIT:(2 * pair + 2) * GQA_UNIT]
            if pair < heads_per_group // 2:
                out = jnp.where(lo, even, pltpu.roll(odd, HEAD_DIM, 1))
            else:
                out = jnp.where(lo, pltpu.roll(even, HEAD_DIM, 1), odd)
            o_ref[pl.ds(r0, GQA_UNIT), pair * LANES:(pair + 1) * LANES] = out.astype(BF16)

    chains = range(GQA_CHAINS)

    def trip(i, carry):
        qt, cmax, acc_prev = [list(x) for x in carry]
        qt_next = [None] * GQA_CHAINS
        units = [i * GQA_CHAINS + k for k in chains]
        m = [jnp.full((1, stacked_rows), NEG_INF, F32) for _ in chains]
        acc = [jnp.zeros((KV_WIDTH + GQA_ONES_ROWS, stacked_rows), F32) for _ in chains]
        for c in range(n_chunks):
            for k in chains:
                m_new = jnp.maximum(m[k], cmax[k])
                if c + 1 < n_chunks:
                    cmax[k] = chunk_scores(c + 1, qt[k], k)
                else:
                    cmax[k] = chunk_scores(0, qt_next[k], k)
                if c == 0:
                    write_out(jnp.maximum(units[k] - GQA_CHAINS, k), acc_prev[k])
                if c == 1:
                    qt_next[k] = prep_q(jnp.minimum(units[k] + GQA_CHAINS, n_units - GQA_CHAINS + k))
                pv = None
                for t in range(GQA_KEY_CHUNK // GQA_KEY_TILE):
                    rows = slice(t * GQA_KEY_TILE, (t + 1) * GQA_KEY_TILE)
                    k0 = c * GQA_KEY_CHUNK + t * GQA_KEY_TILE
                    p = jnp.exp2(s_scr(k, c)[rows, :] - m_new).astype(BF16)
                    d = _dot(vx_ref[:, k0:k0 + GQA_KEY_TILE], p)
                    pv = d if pv is None else pv + d
                acc[k] = acc[k] * jnp.exp2(m[k] - m_new) + pv
                m[k] = m_new
        return tuple(qt_next), tuple(cmax), tuple(acc)

    def trip_bounded(i, carry):
        qt, acc_prev = [list(x) for x in carry]
        qt_next = [None] * GQA_CHAINS
        units = [i * GQA_CHAINS + k for k in chains]
        acc = [None] * GQA_CHAINS
        for c in range(n_chunks):
            rows = slice(c * GQA_KEY_CHUNK, (c + 1) * GQA_KEY_CHUNK)
            for k in chains:
                if c + 1 < n_chunks:
                    chunk_scores(c + 1, qt[k], k, with_max=False)
                else:
                    chunk_scores(0, qt_next[k], k, with_max=False)
                if c == 0:
                    write_out(jnp.maximum(units[k] - GQA_CHAINS, k), acc_prev[k])
                if c == 1:
                    qt_next[k] = prep_q(jnp.minimum(units[k] + GQA_CHAINS, n_units - GQA_CHAINS + k))
                d = _dot(vx_ref[:, rows], jnp.exp2(s_scr(k, c)[...]).astype(BF16))
                acc[k] = d if acc[k] is None else acc[k] + d
        return tuple(qt_next), tuple(acc)

    placeholder = tuple(jnp.ones((KV_WIDTH + GQA_ONES_ROWS, stacked_rows), F32) for _ in chains)

    def run_exact():
        assert n_chunks % 2 == 0
        qt0 = tuple(prep_q(k) for k in chains)
        init = (qt0, tuple(chunk_scores(0, qt0[k], k) for k in chains), placeholder)
        _, _, acc_last = lax.fori_loop(0, n_units // GQA_CHAINS, trip, init)
        for k in chains:
            write_out(n_units - GQA_CHAINS + k, acc_last[k])

    def run_bounded():
        qt0 = tuple(prep_q(k) for k in chains)
        for k in chains:
            chunk_scores(0, qt0[k], k, with_max=False)
        init = (qt0, placeholder)
        _, acc_last = lax.fori_loop(0, n_units // GQA_CHAINS, trip_bounded, init)
        for k in chains:
            write_out(n_units - GQA_CHAINS + k, acc_last[k])

    logit_bound = (HEAD_DIM * scale * GQA_BOUND_SLACK) * jnp.max(jnp.abs(qg_ref[...])) * jnp.max(jnp.abs(kg_ref[...]))
    lax.cond(logit_bound <= GQA_SAFE_LOGIT, run_bounded, run_exact)


def _gqa_attention(q, k, v, cos, sin, qg, kg, l, batch, seq):
    steps = seq // GQA_TQ
    qspec = pl.BlockSpec((GQA_TQ, GQA_WIDTH), lambda b, i: (b * steps + i, 0))
    kvspec = pl.BlockSpec((seq, KV_WIDTH), lambda b, i: (b, 0))
    return pl.pallas_call(
        _gqa_kernel,
        grid=(batch, steps),
        in_specs=[qspec, kvspec, kvspec, _const_spec((seq, LANES)), _const_spec((seq, LANES)),
                  _layer_spec(qg, l), _layer_spec(kg, l)],
        out_specs=qspec,
        out_shape=jax.ShapeDtypeStruct(q.shape, BF16),
        scratch_shapes=[pltpu.VMEM((seq, KV_WIDTH), BF16), pltpu.VMEM((KV_WIDTH + GQA_ONES_ROWS, seq), BF16)]
        + [pltpu.VMEM((GQA_KEY_CHUNK, GQA_Q_HEADS * GQA_UNIT), F32)] * (2 * GQA_CHAINS),
        compiler_params=_params("parallel", "arbitrary"),
        name="gqa_attention",
    )(q, k, v, cos, sin, qg, kg)


def _mem_kv_kernel(m_ref, g_ref, w32_ref, o_ref, w_ref):
    _cast_once([(w32_ref, w_ref)])
    h = _rms(m_ref[...], g_ref[...]).astype(BF16)
    o_ref[...] = _dot(h, w_ref[...]).astype(BF16)


def _mem_kv(mem2, g, w, l):
    n = mem2.shape[0]
    width = w.shape[-1]
    return pl.pallas_call(
        _mem_kv_kernel,
        grid=(n // TM,),
        in_specs=[pl.BlockSpec((TM, D_MODEL), lambda i: (i, 0)), _layer_spec(g, l), _layer_spec(w, l)],
        out_specs=pl.BlockSpec((TM, width), lambda i: (i, 0)),
        out_shape=jax.ShapeDtypeStruct((n, width), BF16),
        scratch_shapes=[pltpu.VMEM(w.shape[1:], BF16)],
        compiler_params=_params("arbitrary"),
        name="mem_kv",
    )(mem2, g, w)


def _post_kernel(x_ref, na_ref, ga_ref, wo32_ref, g_ref, wq32_ref, kv_ref, wmo32_ref, o_ref,
                 wo_ref, wq_ref, wmo_ref):
    _cast_once([(wo32_ref, wo_ref), (wq32_ref, wq_ref), (wmo32_ref, wmo_ref)])
    x1 = x_ref[...] + _dot(na_ref[...], wo_ref[:NA_WIDTH, :]) + _dot(ga_ref[...], wo_ref[NA_WIDTH:, :])
    h = _rms(x1, g_ref[...]).astype(BF16)
    q = (_dot(h, wq_ref[...]) * (MEM_HEAD_DIM ** -0.5)).astype(BF16)
    heads = []
    for hh in range(MEM_HEADS):
        cols = slice(hh * MEM_HEAD_DIM, (hh + 1) * MEM_HEAD_DIM)
        s = _dot_nt(q[:, cols], kv_ref[:, cols])
        m = jnp.max(s, axis=-1, keepdims=True)
        p = jnp.exp(s - m)
        l = jnp.sum(p, axis=-1, keepdims=True)
        vcols = slice(MEM_WIDTH + hh * MEM_HEAD_DIM, MEM_WIDTH + (hh + 1) * MEM_HEAD_DIM)
        heads.append((_dot(p.astype(BF16), kv_ref[:, vcols]) / l).astype(BF16))
    o_ref[...] = x1 + _dot(jnp.concatenate(heads, axis=-1), wmo_ref[...])


def _post(x2, na, ga, wo, g, wq, kv, wmo, l, seq, mem_len):
    n = x2.shape[0]
    tiles_per_seq = seq // TM_PROJ
    row = lambda width: pl.BlockSpec((TM_PROJ, width), lambda i: (i, 0))
    return pl.pallas_call(
        _post_kernel,
        grid=(n // TM_PROJ,),
        in_specs=[row(D_MODEL), row(NA_WIDTH), row(GQA_WIDTH), _layer_spec(wo, l),
                  _layer_spec(g, l), _layer_spec(wq, l),
                  pl.BlockSpec((mem_len, 2 * MEM_WIDTH), lambda i: (i // tiles_per_seq, 0)),
                  _layer_spec(wmo, l)],
        out_specs=row(D_MODEL),
        out_shape=jax.ShapeDtypeStruct((n, D_MODEL), F32),
        scratch_shapes=[pltpu.VMEM(w.shape[1:], BF16) for w in (wo, wq, wmo)],
        compiler_params=_params("arbitrary"),
        name="post_mixer_memattn",
    )(x2, na, ga, wo, g, wq, kv, wmo)


def _ffn_kernel(xp_ref, x_ref, xn_ref, g_ref, wup_ref, cw_ref, cb_ref, wdn_ref, gf_ref, o_ref,
                *, tiles_per_seq, final_norm):
    i = pl.program_id(0)
    first = (i % tiles_per_seq) == 0
    last = (i % tiles_per_seq) == tiles_per_seq - 1
    x = x_ref[...]
    tm = x.shape[0]
    xa = jnp.concatenate([xp_ref[...], x, xn_ref[...]], axis=0)
    ha = _rms(xa, g_ref[...])
    n_all = tm + 2 * HALO
    rid = lax.broadcasted_iota(jnp.int32, (n_all, 1), 0)
    pad = (first & (rid < HALO)) | (last & (rid >= tm + HALO))
    ha = jnp.where(pad, 0.0, ha).astype(BF16)

    def conv(u, c0, width):
        cols = slice(c0, c0 + width)
        prev = pltpu.roll(u, 1, 0)[HALO:HALO + tm]
        nxt = pltpu.roll(u, n_all - 1, 0)[HALO:HALO + tm]
        return (prev * cw_ref[0:1, cols] + u[HALO:HALO + tm] * cw_ref[1:2, cols]
                + nxt * cw_ref[2:3, cols] + cb_ref[:, cols])

    def up(c):
        c0, width = FF_CHUNKS[c]
        return (_dot(ha, wup_ref[:, c0:c0 + width]), _dot(ha, wup_ref[:, D_FF + c0:D_FF + c0 + width]))

    acc = x
    u_next = up(0)
    for c, (c0, width) in enumerate(FF_CHUNKS):
        ug, uv = u_next
        if c + 1 < len(FF_CHUNKS):
            u_next = up(c + 1)
        gate = conv(ug, c0, width)
        val = conv(uv, D_FF + c0, width)
        act = (gate * jax.nn.sigmoid(gate) * val).astype(BF16)
        acc = acc + _dot(act, wdn_ref[c0:c0 + width, :])
    if final_norm:
        acc = _rms(acc, gf_ref[...])
    o_ref[...] = acc


def _ffn(x2, g, wup, cw, cb, wdn, gf, l, seq, final_norm):
    n = x2.shape[0]
    tiles_per_seq = seq // TM_PROJ
    hb = TM_PROJ // HALO
    nh = n // HALO
    return pl.pallas_call(
        functools.partial(_ffn_kernel, tiles_per_seq=tiles_per_seq, final_norm=final_norm),
        grid=(n // TM_PROJ,),
        in_specs=[pl.BlockSpec((HALO, D_MODEL), lambda i: (jnp.maximum(i * hb - 1, 0), 0)),
                  pl.BlockSpec((TM_PROJ, D_MODEL), lambda i: (i, 0)),
                  pl.BlockSpec((HALO, D_MODEL), lambda i: (jnp.minimum((i + 1) * hb, nh - 1), 0)),
                  _layer_spec(g, l), _layer_spec(wup, l), _layer_spec(cw, l),
                  _layer_spec(cb, l), _layer_spec(wdn, l), _const_spec((1, D_MODEL))],
        out_specs=pl.BlockSpec((TM_PROJ, D_MODEL), lambda i: (i, 0)),
        out_shape=jax.ShapeDtypeStruct((n, D_MODEL), F32),
        compiler_params=_params("parallel"),
        name="conv_ffn",
    )(x2, x2, x2, g, wup, cw, cb, wdn, gf)


def kernel(x, mem, norm_mix, w_in, na_rpb, gqa_q_norm, gqa_k_norm, w_out, norm_mem_q, norm_mem_kv,
           w_mem_q, w_mem_kv, w_mem_o, norm_ffn, w_up, conv_w, conv_b, w_down, norm_final):
    batch, seq, d = x.shape
    depth = w_in.shape[0]
    mem_len = mem.shape[1]
    rows = seq // GRID_W
    assert d == D_MODEL and seq % GRID_W == 0 and rows >= NA_KH and rows % NA_ROWS_PER_STEP == 0
    assert seq % TM == 0 and seq % TM_PROJ == 0 and seq % GQA_TQ == 0 and (batch * mem_len) % TM == 0

    cos, sin = _rope_tables(seq)
    x2 = x.reshape(batch * seq, d)
    mem2 = mem.reshape(batch * mem_len, d)
    rows_of = lambda a: a.reshape(a.shape[0], 1, -1)
    lanes2 = lambda a: jnp.tile(a, (1, LANES // HEAD_DIM)).reshape(a.shape[0], 1, LANES)
    norm_mix, norm_mem_q, norm_mem_kv, norm_ffn, conv_b = map(
        rows_of, (norm_mix, norm_mem_q, norm_mem_kv, norm_ffn, conv_b))
    q_gain, k_gain = lanes2(gqa_q_norm), lanes2(gqa_k_norm)
    w_up, w_down = w_up.astype(BF16), w_down.astype(BF16)
    rpb = na_rpb.reshape(depth, -1)

    for l in range(depth):
        naq, nak, nav, gq, gk, gv = _in_proj(x2, norm_mix, w_in, l)
        na = _na_attention(rpb[l], naq, nak, nav, batch, seq)
        ga = _gqa_attention(gq, gk, gv, cos, sin, q_gain, k_gain, l, batch, seq)
        kv = _mem_kv(mem2, norm_mem_kv, w_mem_kv, l)
        x2 = _post(x2, na, ga, w_out, norm_mem_q, w_mem_q, kv, w_mem_o, l, seq, mem_len)
        x2 = _ffn(x2, norm_ffn, w_up, conv_w, conv_b, w_down, norm_final.reshape(1, -1), l, seq,
                  final_norm=(l == depth - 1))
    return x2.reshape(batch, seq, d)
```

```python
import functools

import jax
import jax.numpy as jnp
from jax import lax
from jax.experimental import pallas as pl
from jax.experimental.pallas import tpu as pltpu

D_MODEL = 1024
GRID_W = 64
HEAD_DIM = 64
NA_HEADS = 8
GQA_Q_HEADS = 8
GQA_KV_HEADS = 2
NA_KH = 8
NA_KW = 16
ROPE_THETA = 10000.0
NA_WIDTH = NA_HEADS * HEAD_DIM
GQA_WIDTH = GQA_Q_HEADS * HEAD_DIM
KV_WIDTH = GQA_KV_HEADS * HEAD_DIM
MEM_HEADS = 4
MEM_HEAD_DIM = 128
MEM_WIDTH = MEM_HEADS * MEM_HEAD_DIM
D_FF = 2816
EPS = 1e-6
NEG_INF = -1e30

LANES = 128
SUBLANES = 8
VMEM_LIMIT = 56 * 1024 * 1024

TM = 512
TM_PROJ = 1024
HALO = SUBLANES
MXU_TILE = 256
FF_CHUNKS = ((0, 6 * MXU_TILE), (6 * MXU_TILE, D_FF - 6 * MXU_TILE))
GQA_TQ = 4096
GQA_UNIT = 64
GQA_ONES_ROWS = 16
GQA_KEY_CHUNK = 512
GQA_KEY_TILE = 256
GQA_CHAINS = 2
GQA_SAFE_LOGIT = 100.0
GQA_BOUND_SLACK = 1.02
NA_ROWS_PER_STEP = 16
LOG2_E = 1.4426950408889634

BF16 = jnp.bfloat16
F32 = jnp.float32


def _rms(x, g):
    return x * lax.rsqrt(jnp.mean(x * x, axis=-1, keepdims=True) + EPS) * g


def _dot(a, b):
    return jnp.dot(a, b, preferred_element_type=F32)


def _dot_nt(a, b):
    return lax.dot_general(a, b, (((1,), (1,)), ((), ())), preferred_element_type=F32)


def _const_spec(shape):
    nd = len(shape)
    return pl.BlockSpec(shape, lambda *_: (0,) * nd, pipeline_mode=pl.Buffered(1))


def _layer_spec(stacked, l):
    tail = stacked.shape[1:]
    return pl.BlockSpec((None,) + tail, lambda *_: (l,) + (0,) * len(tail), pipeline_mode=pl.Buffered(1))


def _params(*sem):
    return pltpu.CompilerParams(dimension_semantics=sem, vmem_limit_bytes=VMEM_LIMIT)


def _cast_once(pairs):
    @pl.when(pl.program_id(0) == 0)
    def _():
        for src, dst in pairs:
            dst[...] = src[...].astype(BF16)


def _in_proj_kernel(x_ref, g_ref, w32_ref, naq_ref, nak_ref, nav_ref, gq_ref, gk_ref, gv_ref, w_ref):
    _cast_once([(w32_ref, w_ref)])
    h = _rms(x_ref[...], g_ref[...]).astype(BF16)
    o = 0
    naq_ref[...] = (_dot(h, w_ref[:, o:o + NA_WIDTH]) * (HEAD_DIM ** -0.5 * LOG2_E)).astype(BF16)
    o += NA_WIDTH
    nak_ref[...] = _dot(h, w_ref[:, o:o + NA_WIDTH]).astype(BF16)
    o += NA_WIDTH
    nav_ref[...] = _dot(h, w_ref[:, o:o + NA_WIDTH]).astype(BF16)
    o += NA_WIDTH
    gq_ref[...] = _dot(h, w_ref[:, o:o + GQA_WIDTH])
    o += GQA_WIDTH
    gkv = _dot(h, w_ref[:, o:o + 2 * KV_WIDTH])
    gk_ref[...] = gkv[:, :KV_WIDTH]
    gv_ref[...] = gkv[:, KV_WIDTH:].astype(BF16)


def _in_proj(x2, g, w, l):
    n = x2.shape[0]
    row = lambda width: pl.BlockSpec((TM_PROJ, width), lambda i: (i, 0))
    return pl.pallas_call(
        _in_proj_kernel,
        grid=(n // TM_PROJ,),
        in_specs=[row(D_MODEL), _layer_spec(g, l), _layer_spec(w, l)],
        out_specs=[row(NA_WIDTH), row(NA_WIDTH), row(NA_WIDTH), row(GQA_WIDTH), row(KV_WIDTH), row(KV_WIDTH)],
        out_shape=[
            jax.ShapeDtypeStruct((n, NA_WIDTH), BF16),
            jax.ShapeDtypeStruct((n, NA_WIDTH), BF16),
            jax.ShapeDtypeStruct((n, NA_WIDTH), BF16),
            jax.ShapeDtypeStruct((n, GQA_WIDTH), F32),
            jax.ShapeDtypeStruct((n, KV_WIDTH), F32),
            jax.ShapeDtypeStruct((n, KV_WIDTH), BF16),
        ],
        scratch_shapes=[pltpu.VMEM(w.shape[1:], BF16)],
        compiler_params=_params("arbitrary"),
        name="in_proj",
    )(x2, g, w)


N_ROW_OFF = 2 * NA_KH - 1
N_COL_OFF = 2 * NA_KW - 1


def _build_na_bias(rpb_ref, t_ref, bias_ref):
    lane = lax.broadcasted_iota(jnp.int32, (GRID_W, LANES), 1)
    q = lax.broadcasted_iota(jnp.int32, (GRID_W, LANES), 0)
    k = lane % GRID_W
    col_start = jnp.clip(q - NA_KW // 2, 0, GRID_W - NA_KW)
    valid = (k >= col_start) & (k < col_start + NA_KW)
    lane8 = lax.broadcasted_iota(jnp.int32, (SUBLANES, LANES), 1)
    at = [((lane8 == (d - (NA_KW - 1)) % LANES) | (lane8 == (d - (NA_KW - 1) + GRID_W) % LANES))
          for d in range(N_COL_OFF)]

    def block(b, carry):
        base = jnp.zeros((SUBLANES, LANES), F32)
        for d in range(N_COL_OFF):
            base = jnp.where(at[d], rpb_ref[b * N_COL_OFF + d], base)
        t = pltpu.roll(jnp.tile(base, (GRID_W // SUBLANES, 1)), 0, 1, stride=1, stride_axis=0)
        t_ref[b] = jnp.where(valid, t, NEG_INF) * LOG2_E
        return carry

    lax.fori_loop(0, NA_HEADS * N_ROW_OFF, block, 0, unroll=NA_HEADS)

    lo = lane < GRID_W

    def assemble(cls, carry):
        for h in range(NA_HEADS):
            for ip in range(NA_KH // 2):
                ro = 2 * ip - cls + (NA_KH - 1)
                blk = jnp.where(lo, t_ref[h * N_ROW_OFF + ro], t_ref[h * N_ROW_OFF + ro + 1])
                bias_ref[cls, h // 2, (h % 2) * GRID_W:(h % 2 + 1) * GRID_W, ip * LANES:(ip + 1) * LANES] = blk
        return carry

    lax.fori_loop(0, NA_KH, assemble, 0)


def _na_kernel(rpb_ref, q_ref, k_ref, v_ref, o_ref, t_ref, bias_ref, *, rows):
    rb = pl.program_id(1)

    @pl.when((pl.program_id(0) == 0) & (rb == 0))
    def _():
        _build_na_bias(rpb_ref, t_ref, bias_ref)

    lo = lax.broadcasted_iota(jnp.int32, (GRID_W, LANES), 1) < HEAD_DIM
    win = NA_KH * GRID_W
    pairs = [slice(j * LANES, (j + 1) * LANES) for j in range(NA_HEADS // 2)]
    ones = jnp.ones((win, LANES), BF16)

    def window(i):
        r = rb * NA_ROWS_PER_STEP + i
        row_start = jnp.clip(r - NA_KH // 2, 0, rows - NA_KH)
        return r - row_start, pl.multiple_of(row_start * GRID_W, GRID_W)

    def scores(i):
        cls, kstart = window(i)
        out = []
        for j, cols in enumerate(pairs):
            qp = q_ref[i * GRID_W:(i + 1) * GRID_W, cols]
            zero = jnp.zeros_like(qp)
            qs = jnp.concatenate([jnp.where(lo, qp, zero), jnp.where(lo, zero, qp)], axis=0)
            out.append(_dot_nt(qs, k_ref[pl.ds(kstart, win), cols]) + bias_ref[cls, j])
        return out

    def finish(i, s_list):
        _, kstart = window(i)
        p_list = [jnp.exp2(s - jnp.max(s, axis=-1, keepdims=True)).astype(BF16) for s in s_list]
        for p, cols in zip(p_list, pairs):
            o = _dot(p, jnp.concatenate([v_ref[pl.ds(kstart, win), cols], ones], axis=1))
            o = o[:, :LANES] / o[:, LANES:]
            o_ref[i * GRID_W:(i + 1) * GRID_W, cols] = jnp.where(lo, o[:GRID_W], o[GRID_W:]).astype(BF16)

    s_next = scores(0)
    for i in range(NA_ROWS_PER_STEP):
        s_cur = s_next
        if i + 1 < NA_ROWS_PER_STEP:
            s_next = scores(i + 1)
        finish(i, s_cur)


def _na_attention(rpb, q, k, v, batch, seq):
    rows = seq // GRID_W
    tokens = NA_ROWS_PER_STEP * GRID_W
    steps = rows // NA_ROWS_PER_STEP
    qspec = pl.BlockSpec((tokens, NA_WIDTH), lambda b, r: (b * steps + r, 0))
    kvspec = pl.BlockSpec((seq, NA_WIDTH), lambda b, r: (b, 0))
    return pl.pallas_call(
        functools.partial(_na_kernel, rows=rows),
        grid=(batch, steps),
        in_specs=[pl.BlockSpec(memory_space=pltpu.SMEM), qspec, kvspec, kvspec],
        out_specs=qspec,
        out_shape=jax.ShapeDtypeStruct(q.shape, BF16),
        scratch_shapes=[pltpu.VMEM((NA_HEADS * N_ROW_OFF, GRID_W, LANES), F32),
                        pltpu.VMEM((NA_KH, NA_HEADS // 2, 2 * GRID_W, NA_KH * GRID_W), F32)],
        compiler_params=_params("arbitrary", "arbitrary"),
        name="na_attention",
    )(rpb, q, k, v)


def _rope_tables(seq):
    t = jnp.arange(seq)
    pos = jnp.stack([t // GRID_W, t % GRID_W], axis=-1).astype(F32)
    n_f = HEAD_DIM // 4
    inv_freq = ROPE_THETA ** (-jnp.arange(n_f, dtype=F32) / n_f)
    ang = pos[:, :, None] * inv_freq
    cos, sin = jnp.cos(ang), jnp.sin(ang)
    cos_h = jnp.concatenate([cos, cos], axis=-1).reshape(seq, HEAD_DIM)
    sin_h = jnp.concatenate([-sin, sin], axis=-1).reshape(seq, HEAD_DIM)
    return jnp.tile(cos_h, (1, LANES // HEAD_DIM)), jnp.tile(sin_h, (1, LANES // HEAD_DIM))


def _norm_rope(x, gain, cos, sin, lo, first_half):
    x2 = x * x
    zero = jnp.zeros_like(x2)
    ms_lo = jnp.sum(jnp.where(lo, x2, zero), axis=-1, keepdims=True) * (1.0 / HEAD_DIM)
    ms_hi = jnp.sum(jnp.where(lo, zero, x2), axis=-1, keepdims=True) * (1.0 / HEAD_DIM)
    r = jnp.where(lo, lax.rsqrt(ms_lo + EPS), lax.rsqrt(ms_hi + EPS))
    y = x * r * gain
    n_f = HEAD_DIM // 4
    partner = jnp.where(first_half, pltpu.roll(y, LANES - n_f, 1), pltpu.roll(y, n_f, 1))
    return y * cos + partner * sin


def _gqa_kernel(q_ref, k_ref, v_ref, cos_ref, sin_ref, qg_ref, kg_ref, o_ref, kn_ref, vx_ref, *s_refs):
    qi = pl.program_id(1)
    seq = k_ref.shape[0]
    n_units = GQA_TQ // GQA_UNIT
    n_chunks = seq // GQA_KEY_CHUNK
    heads_per_group = GQA_Q_HEADS // GQA_KV_HEADS
    stacked_rows = GQA_Q_HEADS * GQA_UNIT

    @pl.when(qi == 0)
    def _():
        lane_k = lax.broadcasted_iota(jnp.int32, (seq, LANES), 1)
        kn = _norm_rope(k_ref[...], kg_ref[...], cos_ref[...], sin_ref[...],
                        lane_k < HEAD_DIM, (lane_k % (HEAD_DIM // 2)) < HEAD_DIM // 4)
        kn_ref[...] = kn.astype(BF16)
        vx_ref[:KV_WIDTH, :] = v_ref[...].astype(F32).T.astype(BF16)
        vx_ref[KV_WIDTH:, :] = jnp.ones((GQA_ONES_ROWS, seq), BF16)

    lane = lax.broadcasted_iota(jnp.int32, (GQA_UNIT, LANES), 1)
    lo = lane < HEAD_DIM
    first_half = (lane % (HEAD_DIM // 2)) < HEAD_DIM // 4
    zero = jnp.zeros((GQA_UNIT, LANES), F32)
    scale = HEAD_DIM ** -0.5 * LOG2_E

    def prep_q(u):
        r0 = pl.multiple_of(u * GQA_UNIT, GQA_UNIT)
        t0 = pl.multiple_of(qi * GQA_TQ + u * GQA_UNIT, GQA_UNIT)
        cos = cos_ref[pl.ds(t0, GQA_UNIT), :]
        sin = sin_ref[pl.ds(t0, GQA_UNIT), :]
        stacked = []
        for pair in range(GQA_Q_HEADS // 2):
            qp = q_ref[pl.ds(r0, GQA_UNIT), pair * LANES:(pair + 1) * LANES]
            qn = _norm_rope(qp, qg_ref[...], cos, sin, lo, first_half) * scale
            sw = pltpu.roll(qn, HEAD_DIM, 1)
            if pair < heads_per_group // 2:
                stacked += [jnp.where(lo, qn, zero), jnp.where(lo, sw, zero)]
            else:
                stacked += [jnp.where(lo, zero, sw), jnp.where(lo, zero, qn)]
        return jnp.concatenate(stacked, axis=0).T.astype(BF16)

    def s_scr(k, c):
        return s_refs[2 * k + c % 2]

    def chunk_scores(c, qt, k, with_max=True):
        s = _dot(kn_ref[c * GQA_KEY_CHUNK:(c + 1) * GQA_KEY_CHUNK, :], qt)
        s_scr(k, c)[...] = s
        return jnp.max(s, axis=0, keepdims=True) if with_max else None

    def write_out(u, acc):
        r0 = pl.multiple_of(u * GQA_UNIT, GQA_UNIT)
        o = (acc[:KV_WIDTH] / acc[KV_WIDTH:KV_WIDTH + 1]).T
        for pair in range(GQA_Q_HEADS // 2):
            even = o[(2 * pair) * GQA_UNIT:(2 * pair + 1) * GQA_UNIT]
            odd = o[(2 * pair + 1) * GQA_UNIT:(2 * pair + 2) * GQA_UNIT]
            if pair < heads_per_group // 2:
                out = jnp.where(lo, even, pltpu.roll(odd, HEAD_DIM, 1))
            else:
                out = jnp.where(lo, pltpu.roll(even, HEAD_DIM, 1), odd)
            o_ref[pl.ds(r0, GQA_UNIT), pair * LANES:(pair + 1) * LANES] = out.astype(BF16)

    chains = range(GQA_CHAINS)

    def trip(i, carry):
        qt, cmax, acc_prev = [list(x) for x in carry]
        qt_next = [None] * GQA_CHAINS
        units = [i * GQA_CHAINS + k for k in chains]
        m = [jnp.full((1, stacked_rows), NEG_INF, F32) for _ in chains]
        acc = [jnp.zeros((KV_WIDTH + GQA_ONES_ROWS, stacked_rows), F32) for _ in chains]
        for c in range(n_chunks):
            for k in chains:
                m_new = jnp.maximum(m[k], cmax[k])
                if c + 1 < n_chunks:
                    cmax[k] = chunk_scores(c + 1, qt[k], k)
                else:
                    cmax[k] = chunk_scores(0, qt_next[k], k)
                if c == 0:
                    write_out(jnp.maximum(units[k] - GQA_CHAINS, k), acc_prev[k])
                if c == 1:
                    qt_next[k] = prep_q(jnp.minimum(units[k] + GQA_CHAINS, n_units - GQA_CHAINS + k))
                pv = None
                for t in range(GQA_KEY_CHUNK // GQA_KEY_TILE):
                    rows = slice(t * GQA_KEY_TILE, (t + 1) * GQA_KEY_TILE)
                    k0 = c * GQA_KEY_CHUNK + t * GQA_KEY_TILE
                    p = jnp.exp2(s_scr(k, c)[rows, :] - m_new).astype(BF16)
                    d = _dot(vx_ref[:, k0:k0 + GQA_KEY_TILE], p)
                    pv = d if pv is None else pv + d
                acc[k] = acc[k] * jnp.exp2(m[k] - m_new) + pv
                m[k] = m_new
        return tuple(qt_next), tuple(cmax), tuple(acc)

    def trip_bounded(i, carry):
        qt, acc_prev = [list(x) for x in carry]
        qt_next = [None] * GQA_CHAINS
        units = [i * GQA_CHAINS + k for k in chains]
        acc = [None] * GQA_CHAINS
        for c in range(n_chunks):
            rows = slice(c * GQA_KEY_CHUNK, (c + 1) * GQA_KEY_CHUNK)
            for k in chains:
                if c + 1 < n_chunks:
                    chunk_scores(c + 1, qt[k], k, with_max=False)
                else:
                    chunk_scores(0, qt_next[k], k, with_max=False)
                if c == 0:
                    write_out(jnp.maximum(units[k] - GQA_CHAINS, k), acc_prev[k])
                if c == 1:
                    qt_next[k] = prep_q(jnp.minimum(units[k] + GQA_CHAINS, n_units - GQA_CHAINS + k))
                d = _dot(vx_ref[:, rows], jnp.exp2(s_scr(k, c)[...]).astype(BF16))
                acc[k] = d if acc[k] is None else acc[k] + d
        return tuple(qt_next), tuple(acc)

    placeholder = tuple(jnp.ones((KV_WIDTH + GQA_ONES_ROWS, stacked_rows), F32) for _ in chains)

    def run_exact():
        assert n_chunks % 2 == 0
        qt0 = tuple(prep_q(k) for k in chains)
        init = (qt0, tuple(chunk_scores(0, qt0[k], k) for k in chains), placeholder)
        _, _, acc_last = lax.fori_loop(0, n_units // GQA_CHAINS, trip, init)
        for k in chains:
            write_out(n_units - GQA_CHAINS + k, acc_last[k])

    def run_bounded():
        qt0 = tuple(prep_q(k) for k in chains)
        for k in chains:
            chunk_scores(0, qt0[k], k, with_max=False)
        init = (qt0, placeholder)
        _, acc_last = lax.fori_loop(0, n_units // GQA_CHAINS, trip_bounded, init)
        for k in chains:
            write_out(n_units - GQA_CHAINS + k, acc_last[k])

    logit_bound = (HEAD_DIM * scale * GQA_BOUND_SLACK) * jnp.max(jnp.abs(qg_ref[...])) * jnp.max(jnp.abs(kg_ref[...]))
    lax.cond(logit_bound <= GQA_SAFE_LOGIT, run_bounded, run_exact)


def _gqa_attention(q, k, v, cos, sin, qg, kg, l, batch, seq):
    steps = seq // GQA_TQ
    qspec = pl.BlockSpec((GQA_TQ, GQA_WIDTH), lambda b, i: (b * steps + i, 0))
    kvspec = pl.BlockSpec((seq, KV_WIDTH), lambda b, i: (b, 0))
    return pl.pallas_call(
        _gqa_kernel,
        grid=(batch, steps),
        in_specs=[qspec, kvspec, kvspec, _const_spec((seq, LANES)), _const_spec((seq, LANES)),
                  _layer_spec(qg, l), _layer_spec(kg, l)],
        out_specs=qspec,
        out_shape=jax.ShapeDtypeStruct(q.shape, BF16),
        scratch_shapes=[pltpu.VMEM((seq, KV_WIDTH), BF16), pltpu.VMEM((KV_WIDTH + GQA_ONES_ROWS, seq), BF16)]
        + [pltpu.VMEM((GQA_KEY_CHUNK, GQA_Q_HEADS * GQA_UNIT), F32)] * (2 * GQA_CHAINS),
        compiler_params=_params("parallel", "arbitrary"),
        name="gqa_attention",
    )(q, k, v, cos, sin, qg, kg)


def _mem_kv_kernel(m_ref, g_ref, w32_ref, o_ref, w_ref):
    _cast_once([(w32_ref, w_ref)])
    h = _rms(m_ref[...], g_ref[...]).astype(BF16)
    o_ref[...] = _dot(h, w_ref[...]).astype(BF16)


def _mem_kv(mem2, g, w, l):
    n = mem2.shape[0]
    width = w.shape[-1]
    return pl.pallas_call(
        _mem_kv_kernel,
        grid=(n // TM,),
        in_specs=[pl.BlockSpec((TM, D_MODEL), lambda i: (i, 0)), _layer_spec(g, l), _layer_spec(w, l)],
        out_specs=pl.BlockSpec((TM, width), lambda i: (i, 0)),
        out_shape=jax.ShapeDtypeStruct((n, width), BF16),
        scratch_shapes=[pltpu.VMEM(w.shape[1:], BF16)],
        compiler_params=_params("arbitrary"),
        name="mem_kv",
    )(mem2, g, w)


def _post_kernel(x_ref, na_ref, ga_ref, wo32_ref, g_ref, wq32_ref, kv_ref, wmo32_ref, wup32_ref, wdn32_ref,
                 o_ref, wup_ref, wdn_ref, wo_ref, wq_ref, wmo_ref):
    _cast_once([(wo32_ref, wo_ref), (wq32_ref, wq_ref), (wmo32_ref, wmo_ref)])
    wup_ref[...] = wup32_ref[...].astype(BF16)
    wdn_ref[...] = wdn32_ref[...].astype(BF16)
    x1 = x_ref[...] + _dot(na_ref[...], wo_ref[:NA_WIDTH, :]) + _dot(ga_ref[...], wo_ref[NA_WIDTH:, :])
    h = _rms(x1, g_ref[...]).astype(BF16)
    q = (_dot(h, wq_ref[...]) * (MEM_HEAD_DIM ** -0.5)).astype(BF16)
    heads = []
    for hh in range(MEM_HEADS):
        cols = slice(hh * MEM_HEAD_DIM, (hh + 1) * MEM_HEAD_DIM)
        s = _dot_nt(q[:, cols], kv_ref[:, cols])
        m = jnp.max(s, axis=-1, keepdims=True)
        p = jnp.exp(s - m)
        l = jnp.sum(p, axis=-1, keepdims=True)
        vcols = slice(MEM_WIDTH + hh * MEM_HEAD_DIM, MEM_WIDTH + (hh + 1) * MEM_HEAD_DIM)
        heads.append((_dot(p.astype(BF16), kv_ref[:, vcols]) / l).astype(BF16))
    o_ref[...] = x1 + _dot(jnp.concatenate(heads, axis=-1), wmo_ref[...])


def _post(x2, na, ga, wo, g, wq, kv, wmo, wup, wdn, l, seq, mem_len):
    n = x2.shape[0]
    steps = n // TM_PROJ
    tiles_per_seq = seq // TM_PROJ
    row = lambda width: pl.BlockSpec((TM_PROJ, width), lambda i: (i, 0))
    up_rows, dn_rows = wup.shape[1] // steps, wdn.shape[1] // steps
    assert up_rows * steps == wup.shape[1] and dn_rows * steps == wdn.shape[1]
    return pl.pallas_call(
        _post_kernel,
        grid=(steps,),
        in_specs=[row(D_MODEL), row(NA_WIDTH), row(GQA_WIDTH), _layer_spec(wo, l),
                  _layer_spec(g, l), _layer_spec(wq, l),
                  pl.BlockSpec((mem_len, 2 * MEM_WIDTH), lambda i: (i // tiles_per_seq, 0)),
                  _layer_spec(wmo, l),
                  pl.BlockSpec((None, up_rows, wup.shape[2]), lambda i: (l, i, 0)),
                  pl.BlockSpec((None, dn_rows, wdn.shape[2]), lambda i: (l, i, 0))],
        out_specs=[row(D_MODEL),
                   pl.BlockSpec((up_rows, wup.shape[2]), lambda i: (i, 0)),
                   pl.BlockSpec((dn_rows, wdn.shape[2]), lambda i: (i, 0))],
        out_shape=[jax.ShapeDtypeStruct((n, D_MODEL), F32),
                   jax.ShapeDtypeStruct(wup.shape[1:], BF16),
                   jax.ShapeDtypeStruct(wdn.shape[1:], BF16)],
        scratch_shapes=[pltpu.VMEM(w.shape[1:], BF16) for w in (wo, wq, wmo)],
        compiler_params=_params("arbitrary"),
        name="post_mixer_memattn",
    )(x2, na, ga, wo, g, wq, kv, wmo, wup, wdn)


def _ffn_kernel(xp_ref, x_ref, xn_ref, g_ref, wup_ref, cw_ref, cb_ref, wdn_ref, gf_ref, o_ref,
                *, tiles_per_seq, final_norm):
    i = pl.program_id(0)
    first = (i % tiles_per_seq) == 0
    last = (i % tiles_per_seq) == tiles_per_seq - 1
    x = x_ref[...]
    tm = x.shape[0]
    xa = jnp.concatenate([xp_ref[...], x, xn_ref[...]], axis=0)
    ha = _rms(xa, g_ref[...])
    n_all = tm + 2 * HALO
    rid = lax.broadcasted_iota(jnp.int32, (n_all, 1), 0)
    pad = (first & (rid < HALO)) | (last & (rid >= tm + HALO))
    ha = jnp.where(pad, 0.0, ha).astype(BF16)

    def conv(u, c0, width):
        cols = slice(c0, c0 + width)
        prev = pltpu.roll(u, 1, 0)[HALO:HALO + tm]
        nxt = pltpu.roll(u, n_all - 1, 0)[HALO:HALO + tm]
        return (prev * cw_ref[0:1, cols] + u[HALO:HALO + tm] * cw_ref[1:2, cols]
                + nxt * cw_ref[2:3, cols] + cb_ref[:, cols])

    def up(c):
        c0, width = FF_CHUNKS[c]
        return (_dot(ha, wup_ref[:, c0:c0 + width]), _dot(ha, wup_ref[:, D_FF + c0:D_FF + c0 + width]))

    acc = x
    u_next = up(0)
    for c, (c0, width) in enumerate(FF_CHUNKS):
        ug, uv = u_next
        if c + 1 < len(FF_CHUNKS):
            u_next = up(c + 1)
        gate = conv(ug, c0, width)
        val = conv(uv, D_FF + c0, width)
        act = (gate * jax.nn.sigmoid(gate) * val).astype(BF16)
        acc = acc + _dot(act, wdn_ref[c0:c0 + width, :])
    if final_norm:
        acc = _rms(acc, gf_ref[...])
    o_ref[...] = acc


def _ffn(x2, g, wup, cw, cb, wdn, gf, l, seq, final_norm):
    n = x2.shape[0]
    tiles_per_seq = seq // TM_PROJ
    hb = TM_PROJ // HALO
    nh = n // HALO
    return pl.pallas_call(
        functools.partial(_ffn_kernel, tiles_per_seq=tiles_per_seq, final_norm=final_norm),
        grid=(n // TM_PROJ,),
        in_specs=[pl.BlockSpec((HALO, D_MODEL), lambda i: (jnp.maximum(i * hb - 1, 0), 0)),
                  pl.BlockSpec((TM_PROJ, D_MODEL), lambda i: (i, 0)),
                  pl.BlockSpec((HALO, D_MODEL), lambda i: (jnp.minimum((i + 1) * hb, nh - 1), 0)),
                  _layer_spec(g, l), _const_spec(wup.shape), _layer_spec(cw, l),
                  _layer_spec(cb, l), _const_spec(wdn.shape), _const_spec((1, D_MODEL))],
        out_specs=pl.BlockSpec((TM_PROJ, D_MODEL), lambda i: (i, 0)),
        out_shape=jax.ShapeDtypeStruct((n, D_MODEL), F32),
        compiler_params=_params("parallel"),
        name="conv_ffn",
    )(x2, x2, x2, g, wup, cw, cb, wdn, gf)


def kernel(x, mem, norm_mix, w_in, na_rpb, gqa_q_norm, gqa_k_norm, w_out, norm_mem_q, norm_mem_kv,
           w_mem_q, w_mem_kv, w_mem_o, norm_ffn, w_up, conv_w, conv_b, w_down, norm_final):
    batch, seq, d = x.shape
    depth = w_in.shape[0]
    mem_len = mem.shape[1]
    rows = seq // GRID_W
    assert d == D_MODEL and seq % GRID_W == 0 and rows >= NA_KH and rows % NA_ROWS_PER_STEP == 0
    assert seq % TM == 0 and seq % TM_PROJ == 0 and seq % GQA_TQ == 0 and (batch * mem_len) % TM == 0

    cos, sin = _rope_tables(seq)
    x2 = x.reshape(batch * seq, d)
    mem2 = mem.reshape(batch * mem_len, d)
    rows_of = lambda a: a.reshape(a.shape[0], 1, -1)
    lanes2 = lambda a: jnp.tile(a, (1, LANES // HEAD_DIM)).reshape(a.shape[0], 1, LANES)
    norm_mix, norm_mem_q, norm_mem_kv, norm_ffn, conv_b = map(
        rows_of, (norm_mix, norm_mem_q, norm_mem_kv, norm_ffn, conv_b))
    q_gain, k_gain = lanes2(gqa_q_norm), lanes2(gqa_k_norm)
    rpb = na_rpb.reshape(depth, -1)

    for l in range(depth):
        naq, nak, nav, gq, gk, gv = _in_proj(x2, norm_mix, w_in, l)
        na = _na_attention(rpb[l], naq, nak, nav, batch, seq)
        ga = _gqa_attention(gq, gk, gv, cos, sin, q_gain, k_gain, l, batch, seq)
        kv = _mem_kv(mem2, norm_mem_kv, w_mem_kv, l)
        x2, w_up_l, w_down_l = _post(x2, na, ga, w_out, norm_mem_q, w_mem_q, kv, w_mem_o, w_up, w_down,
                                     l, seq, mem_len)
        x2 = _ffn(x2, norm_ffn, w_up_l, conv_w, conv_b, w_down_l, norm_final.reshape(1, -1), l, seq,
                  final_norm=(l == depth - 1))
    return x2.reshape(batch, seq, d)
```

```python
import functools

import jax
import jax.numpy as jnp
from jax import lax
from jax.experimental import pallas as pl
from jax.experimental.pallas import tpu as pltpu

D_MODEL = 1024
GRID_W = 64
HEAD_DIM = 64
NA_HEADS = 8
GQA_Q_HEADS = 8
GQA_KV_HEADS = 2
NA_KH = 8
NA_KW = 16
ROPE_THETA = 10000.0
NA_WIDTH = NA_HEADS * HEAD_DIM
GQA_WIDTH = GQA_Q_HEADS * HEAD_DIM
KV_WIDTH = GQA_KV_HEADS * HEAD_DIM
MEM_HEADS = 4
MEM_HEAD_DIM = 128
MEM_WIDTH = MEM_HEADS * MEM_HEAD_DIM
D_FF = 2816
EPS = 1e-6
NEG_INF = -1e30

LANES = 128
SUBLANES = 8
VMEM_LIMIT = 56 * 1024 * 1024

TM_PROJ = 1024
HALO = SUBLANES
MXU_TILE = 256
FF_CHUNKS = ((0, 6 * MXU_TILE), (6 * MXU_TILE, D_FF - 6 * MXU_TILE))
GQA_TQ = 4096
GQA_UNIT = 64
GQA_ONES_ROWS = 16
GQA_KEY_CHUNK = 512
GQA_KEY_TILE = 256
GQA_CHAINS = 2
GQA_SAFE_LOGIT = 100.0
GQA_BOUND_SLACK = 1.02
NA_ROWS_PER_STEP = 16
LOG2_E = 1.4426950408889634

BF16 = jnp.bfloat16
F32 = jnp.float32


def _rms(x, g):
    return x * lax.rsqrt(jnp.mean(x * x, axis=-1, keepdims=True) + EPS) * g


def _dot(a, b):
    return jnp.dot(a, b, preferred_element_type=F32)


def _dot_nt(a, b):
    return lax.dot_general(a, b, (((1,), (1,)), ((), ())), preferred_element_type=F32)


def _const_spec(shape):
    nd = len(shape)
    return pl.BlockSpec(shape, lambda *_: (0,) * nd, pipeline_mode=pl.Buffered(1))


def _layer_spec(stacked, l):
    tail = stacked.shape[1:]
    return pl.BlockSpec((None,) + tail, lambda *_: (l,) + (0,) * len(tail), pipeline_mode=pl.Buffered(1))


def _params(*sem):
    return pltpu.CompilerParams(dimension_semantics=sem, vmem_limit_bytes=VMEM_LIMIT)


def _cast_once(pairs):
    @pl.when(pl.program_id(0) == 0)
    def _():
        for src, dst in pairs:
            dst[...] = src[...].astype(BF16)


def _in_proj_kernel(x_ref, g_ref, w32_ref, naq_ref, nak_ref, nav_ref, gq_ref, gk_ref, gv_ref, w_ref):
    _cast_once([(w32_ref, w_ref)])
    h = _rms(x_ref[...], g_ref[...]).astype(BF16)
    o = 0
    naq_ref[...] = (_dot(h, w_ref[:, o:o + NA_WIDTH]) * (HEAD_DIM ** -0.5 * LOG2_E)).astype(BF16)
    o += NA_WIDTH
    nak_ref[...] = _dot(h, w_ref[:, o:o + NA_WIDTH]).astype(BF16)
    o += NA_WIDTH
    nav_ref[...] = _dot(h, w_ref[:, o:o + NA_WIDTH]).astype(BF16)
    o += NA_WIDTH
    gq_ref[...] = _dot(h, w_ref[:, o:o + GQA_WIDTH])
    o += GQA_WIDTH
    gkv = _dot(h, w_ref[:, o:o + 2 * KV_WIDTH])
    gk_ref[...] = gkv[:, :KV_WIDTH]
    gv_ref[...] = gkv[:, KV_WIDTH:].astype(BF16)


def _in_proj(x2, g, w, l):
    n = x2.shape[0]
    row = lambda width: pl.BlockSpec((TM_PROJ, width), lambda i: (i, 0))
    return pl.pallas_call(
        _in_proj_kernel,
        grid=(n // TM_PROJ,),
        in_specs=[row(D_MODEL), _layer_spec(g, l), _layer_spec(w, l)],
        out_specs=[row(NA_WIDTH), row(NA_WIDTH), row(NA_WIDTH), row(GQA_WIDTH), row(KV_WIDTH), row(KV_WIDTH)],
        out_shape=[
            jax.ShapeDtypeStruct((n, NA_WIDTH), BF16),
            jax.ShapeDtypeStruct((n, NA_WIDTH), BF16),
            jax.ShapeDtypeStruct((n, NA_WIDTH), BF16),
            jax.ShapeDtypeStruct((n, GQA_WIDTH), F32),
            jax.ShapeDtypeStruct((n, KV_WIDTH), F32),
            jax.ShapeDtypeStruct((n, KV_WIDTH), BF16),
        ],
        scratch_shapes=[pltpu.VMEM(w.shape[1:], BF16)],
        compiler_params=_params("arbitrary"),
        name="in_proj",
    )(x2, g, w)


N_ROW_OFF = 2 * NA_KH - 1
N_COL_OFF = 2 * NA_KW - 1


def _build_na_bias(rpb_ref, t_ref, bias_ref):
    lane = lax.broadcasted_iota(jnp.int32, (GRID_W, LANES), 1)
    q = lax.broadcasted_iota(jnp.int32, (GRID_W, LANES), 0)
    k = lane % GRID_W
    col_start = jnp.clip(q - NA_KW // 2, 0, GRID_W - NA_KW)
    valid = (k >= col_start) & (k < col_start + NA_KW)
    lane8 = lax.broadcasted_iota(jnp.int32, (SUBLANES, LANES), 1)
    at = [((lane8 == (d - (NA_KW - 1)) % LANES) | (lane8 == (d - (NA_KW - 1) + GRID_W) % LANES))
          for d in range(N_COL_OFF)]

    def block(b, carry):
        base = jnp.zeros((SUBLANES, LANES), F32)
        for d in range(N_COL_OFF):
            base = jnp.where(at[d], rpb_ref[b * N_COL_OFF + d], base)
        t = pltpu.roll(jnp.tile(base, (GRID_W // SUBLANES, 1)), 0, 1, stride=1, stride_axis=0)
        t_ref[b] = jnp.where(valid, t, NEG_INF) * LOG2_E
        return carry

    lax.fori_loop(0, NA_HEADS * N_ROW_OFF, block, 0, unroll=NA_HEADS)

    lo = lane < GRID_W

    def assemble(cls, carry):
        for h in range(NA_HEADS):
            for ip in range(NA_KH // 2):
                ro = 2 * ip - cls + (NA_KH - 1)
                blk = jnp.where(lo, t_ref[h * N_ROW_OFF + ro], t_ref[h * N_ROW_OFF + ro + 1])
                bias_ref[cls, h // 2, (h % 2) * GRID_W:(h % 2 + 1) * GRID_W, ip * LANES:(ip + 1) * LANES] = blk
        return carry

    lax.fori_loop(0, NA_KH, assemble, 0)


def _na_kernel(rpb_ref, q_ref, k_ref, v_ref, o_ref, t_ref, bias_ref, *, rows):
    rb = pl.program_id(1)

    @pl.when((pl.program_id(0) == 0) & (rb == 0))
    def _():
        _build_na_bias(rpb_ref, t_ref, bias_ref)

    lo = lax.broadcasted_iota(jnp.int32, (GRID_W, LANES), 1) < HEAD_DIM
    win = NA_KH * GRID_W
    pairs = [slice(j * LANES, (j + 1) * LANES) for j in range(NA_HEADS // 2)]
    ones = jnp.ones((win, LANES), BF16)

    def window(i):
        r = rb * NA_ROWS_PER_STEP + i
        row_start = jnp.clip(r - NA_KH // 2, 0, rows - NA_KH)
        return r - row_start, pl.multiple_of(row_start * GRID_W, GRID_W)

    def scores(i):
        cls, kstart = window(i)
        out = []
        for j, cols in enumerate(pairs):
            qp = q_ref[i * GRID_W:(i + 1) * GRID_W, cols]
            zero = jnp.zeros_like(qp)
            qs = jnp.concatenate([jnp.where(lo, qp, zero), jnp.where(lo, zero, qp)], axis=0)
            out.append(_dot_nt(qs, k_ref[pl.ds(kstart, win), cols]) + bias_ref[cls, j])
        return out

    def finish(i, s_list):
        _, kstart = window(i)
        p_list = [jnp.exp2(s - jnp.max(s, axis=-1, keepdims=True)).astype(BF16) for s in s_list]
        for p, cols in zip(p_list, pairs):
            o = _dot(p, jnp.concatenate([v_ref[pl.ds(kstart, win), cols], ones], axis=1))
            o = o[:, :LANES] / o[:, LANES:]
            o_ref[i * GRID_W:(i + 1) * GRID_W, cols] = jnp.where(lo, o[:GRID_W], o[GRID_W:]).astype(BF16)

    s_next = scores(0)
    for i in range(NA_ROWS_PER_STEP):
        s_cur = s_next
        if i + 1 < NA_ROWS_PER_STEP:
            s_next = scores(i + 1)
        finish(i, s_cur)


def _na_attention(rpb, q, k, v, batch, seq):
    rows = seq // GRID_W
    tokens = NA_ROWS_PER_STEP * GRID_W
    steps = rows // NA_ROWS_PER_STEP
    qspec = pl.BlockSpec((tokens, NA_WIDTH), lambda b, r: (b * steps + r, 0))
    kvspec = pl.BlockSpec((seq, NA_WIDTH), lambda b, r: (b, 0))
    return pl.pallas_call(
        functools.partial(_na_kernel, rows=rows),
        grid=(batch, steps),
        in_specs=[pl.BlockSpec(memory_space=pltpu.SMEM), qspec, kvspec, kvspec],
        out_specs=qspec,
        out_shape=jax.ShapeDtypeStruct(q.shape, BF16),
        scratch_shapes=[pltpu.VMEM((NA_HEADS * N_ROW_OFF, GRID_W, LANES), F32),
                        pltpu.VMEM((NA_KH, NA_HEADS // 2, 2 * GRID_W, NA_KH * GRID_W), F32)],
        compiler_params=_params("arbitrary", "arbitrary"),
        name="na_attention",
    )(rpb, q, k, v)


def _rope_tables(seq):
    t = jnp.arange(seq)
    pos = jnp.stack([t // GRID_W, t % GRID_W], axis=-1).astype(F32)
    n_f = HEAD_DIM // 4
    inv_freq = ROPE_THETA ** (-jnp.arange(n_f, dtype=F32) / n_f)
    ang = pos[:, :, None] * inv_freq
    cos, sin = jnp.cos(ang), jnp.sin(ang)
    cos_h = jnp.concatenate([cos, cos], axis=-1).reshape(seq, HEAD_DIM)
    sin_h = jnp.concatenate([-sin, sin], axis=-1).reshape(seq, HEAD_DIM)
    return jnp.tile(cos_h, (1, LANES // HEAD_DIM)), jnp.tile(sin_h, (1, LANES // HEAD_DIM))


def _norm_rope(x, gain, cos, sin, lo, first_half):
    x2 = x * x
    zero = jnp.zeros_like(x2)
    ms_lo = jnp.sum(jnp.where(lo, x2, zero), axis=-1, keepdims=True) * (1.0 / HEAD_DIM)
    ms_hi = jnp.sum(jnp.where(lo, zero, x2), axis=-1, keepdims=True) * (1.0 / HEAD_DIM)
    r = jnp.where(lo, lax.rsqrt(ms_lo + EPS), lax.rsqrt(ms_hi + EPS))
    y = x * r * gain
    n_f = HEAD_DIM // 4
    partner = jnp.where(first_half, pltpu.roll(y, LANES - n_f, 1), pltpu.roll(y, n_f, 1))
    return y * cos + partner * sin


def _gqa_kernel(q_ref, k_ref, v_ref, cos_ref, sin_ref, qg_ref, kg_ref, o_ref, kn_ref, vx_ref, *s_refs):
    qi = pl.program_id(1)
    seq = k_ref.shape[0]
    n_units = GQA_TQ // GQA_UNIT
    n_chunks = seq // GQA_KEY_CHUNK
    heads_per_group = GQA_Q_HEADS // GQA_KV_HEADS
    stacked_rows = GQA_Q_HEADS * GQA_UNIT

    @pl.when(qi == 0)
    def _():
        lane_k = lax.broadcasted_iota(jnp.int32, (seq, LANES), 1)
        kn = _norm_rope(k_ref[...], kg_ref[...], cos_ref[...], sin_ref[...],
                        lane_k < HEAD_DIM, (lane_k % (HEAD_DIM // 2)) < HEAD_DIM // 4)
        kn_ref[...] = kn.astype(BF16)
        vx_ref[:KV_WIDTH, :] = v_ref[...].astype(F32).T.astype(BF16)
        vx_ref[KV_WIDTH:, :] = jnp.ones((GQA_ONES_ROWS, seq), BF16)

    lane = lax.broadcasted_iota(jnp.int32, (GQA_UNIT, LANES), 1)
    lo = lane < HEAD_DIM
    first_half = (lane % (HEAD_DIM // 2)) < HEAD_DIM // 4
    zero = jnp.zeros((GQA_UNIT, LANES), F32)
    scale = HEAD_DIM ** -0.5 * LOG2_E

    def prep_q(u):
        r0 = pl.multiple_of(u * GQA_UNIT, GQA_UNIT)
        t0 = pl.multiple_of(qi * GQA_TQ + u * GQA_UNIT, GQA_UNIT)
        cos = cos_ref[pl.ds(t0, GQA_UNIT), :]
        sin = sin_ref[pl.ds(t0, GQA_UNIT), :]
        stacked = []
        for pair in range(GQA_Q_HEADS // 2):
            qp = q_ref[pl.ds(r0, GQA_UNIT), pair * LANES:(pair + 1) * LANES]
            qn = _norm_rope(qp, qg_ref[...], cos, sin, lo, first_half) * scale
            sw = pltpu.roll(qn, HEAD_DIM, 1)
            if pair < heads_per_group // 2:
                stacked += [jnp.where(lo, qn, zero), jnp.where(lo, sw, zero)]
            else:
                stacked += [jnp.where(lo, zero, sw), jnp.where(lo, zero, qn)]
        return jnp.concatenate(stacked, axis=0).T.astype(BF16)

    def s_scr(k, c):
        return s_refs[2 * k + c % 2]

    def chunk_scores(c, qt, k, with_max=True):
        s = _dot(kn_ref[c * GQA_KEY_CHUNK:(c + 1) * GQA_KEY_CHUNK, :], qt)
        s_scr(k, c)[...] = s
        return jnp.max(s, axis=0, keepdims=True) if with_max else None

    def write_out(u, acc):
        r0 = pl.multiple_of(u * GQA_UNIT, GQA_UNIT)
        o = (acc[:KV_WIDTH] / acc[KV_WIDTH:KV_WIDTH + 1]).T
        for pair in range(GQA_Q_HEADS // 2):
            even = o[(2 * pair) * GQA_UNIT:(2 * pair + 1) * GQA_UNIT]
            odd = o[(2 * pair + 1) * GQA_UNIT:(2 * pair + 2) * GQA_UNIT]
            if pair < heads_per_group // 2:
                out = jnp.where(lo, even, pltpu.roll(odd, HEAD_DIM, 1))
            else:
                out = jnp.where(lo, pltpu.roll(even, HEAD_DIM, 1), odd)
            o_ref[pl.ds(r0, GQA_UNIT), pair * LANES:(pair + 1) * LANES] = out.astype(BF16)

    chains = range(GQA_CHAINS)

    def trip(i, carry):
        qt, cmax, acc_prev = [list(x) for x in carry]
        qt_next = [None] * GQA_CHAINS
        units = [i * GQA_CHAINS + k for k in chains]
        m = [jnp.full((1, stacked_rows), NEG_INF, F32) for _ in chains]
        acc = [jnp.zeros((KV_WIDTH + GQA_ONES_ROWS, stacked_rows), F32) for _ in chains]
        for c in range(n_chunks):
            for k in chains:
                m_new = jnp.maximum(m[k], cmax[k])
                if c + 1 < n_chunks:
                    cmax[k] = chunk_scores(c + 1, qt[k], k)
                else:
                    cmax[k] = chunk_scores(0, qt_next[k], k)
                if c == 0:
                    write_out(jnp.maximum(units[k] - GQA_CHAINS, k), acc_prev[k])
                if c == 1:
                    qt_next[k] = prep_q(jnp.minimum(units[k] + GQA_CHAINS, n_units - GQA_CHAINS + k))
                pv = None
                for t in range(GQA_KEY_CHUNK // GQA_KEY_TILE):
                    rows = slice(t * GQA_KEY_TILE, (t + 1) * GQA_KEY_TILE)
                    k0 = c * GQA_KEY_CHUNK + t * GQA_KEY_TILE
                    p = jnp.exp2(s_scr(k, c)[rows, :] - m_new).astype(BF16)
                    d = _dot(vx_ref[:, k0:k0 + GQA_KEY_TILE], p)
                    pv = d if pv is None else pv + d
                acc[k] = acc[k] * jnp.exp2(m[k] - m_new) + pv
                m[k] = m_new
        return tuple(qt_next), tuple(cmax), tuple(acc)

    def trip_bounded(i, carry):
        qt, acc_prev = [list(x) for x in carry]
        qt_next = [None] * GQA_CHAINS
        units = [i * GQA_CHAINS + k for k in chains]
        acc = [None] * GQA_CHAINS
        for c in range(n_chunks):
            rows = slice(c * GQA_KEY_CHUNK, (c + 1) * GQA_KEY_CHUNK)
            for k in chains:
                if c + 1 < n_chunks:
                    chunk_scores(c + 1, qt[k], k, with_max=False)
                else:
                    chunk_scores(0, qt_next[k], k, with_max=False)
                if c == 0:
                    write_out(jnp.maximum(units[k] - GQA_CHAINS, k), acc_prev[k])
                if c == 1:
                    qt_next[k] = prep_q(jnp.minimum(units[k] + GQA_CHAINS, n_units - GQA_CHAINS + k))
                d = _dot(vx_ref[:, rows], jnp.exp2(s_scr(k, c)[...]).astype(BF16))
                acc[k] = d if acc[k] is None else acc[k] + d
        return tuple(qt_next), tuple(acc)

    placeholder = tuple(jnp.ones((KV_WIDTH + GQA_ONES_ROWS, stacked_rows), F32) for _ in chains)

    def run_exact():
        assert n_chunks % 2 == 0
        qt0 = tuple(prep_q(k) for k in chains)
        init = (qt0, tuple(chunk_scores(0, qt0[k], k) for k in chains), placeholder)
        _, _, acc_last = lax.fori_loop(0, n_units // GQA_CHAINS, trip, init)
        for k in chains:
            write_out(n_units - GQA_CHAINS + k, acc_last[k])

    def run_bounded():
        qt0 = tuple(prep_q(k) for k in chains)
        for k in chains:
            chunk_scores(0, qt0[k], k, with_max=False)
        init = (qt0, placeholder)
        _, acc_last = lax.fori_loop(0, n_units // GQA_CHAINS, trip_bounded, init)
        for k in chains:
            write_out(n_units - GQA_CHAINS + k, acc_last[k])

    logit_bound = (HEAD_DIM * scale * GQA_BOUND_SLACK) * jnp.max(jnp.abs(qg_ref[...])) * jnp.max(jnp.abs(kg_ref[...]))
    lax.cond(logit_bound <= GQA_SAFE_LOGIT, run_bounded, run_exact)


def _gqa_attention(q, k, v, cos, sin, qg, kg, l, batch, seq):
    steps = seq // GQA_TQ
    qspec = pl.BlockSpec((GQA_TQ, GQA_WIDTH), lambda b, i: (b * steps + i, 0))
    kvspec = pl.BlockSpec((seq, KV_WIDTH), lambda b, i: (b, 0))
    return pl.pallas_call(
        _gqa_kernel,
        grid=(batch, steps),
        in_specs=[qspec, kvspec, kvspec, _const_spec((seq, LANES)), _const_spec((seq, LANES)),
                  _layer_spec(qg, l), _layer_spec(kg, l)],
        out_specs=qspec,
        out_shape=jax.ShapeDtypeStruct(q.shape, BF16),
        scratch_shapes=[pltpu.VMEM((seq, KV_WIDTH), BF16), pltpu.VMEM((KV_WIDTH + GQA_ONES_ROWS, seq), BF16)]
        + [pltpu.VMEM((GQA_KEY_CHUNK, GQA_Q_HEADS * GQA_UNIT), F32)] * (2 * GQA_CHAINS),
        compiler_params=_params("parallel", "arbitrary"),
        name="gqa_attention",
    )(q, k, v, cos, sin, qg, kg)


def _mem_kv_kernel(m_ref, g_ref, w32_ref, o_ref):
    h = _rms(m_ref[...], g_ref[...]).astype(BF16)
    o_ref[...] = _dot(h, w32_ref[...].astype(BF16)).astype(BF16)


def _mem_kv(mem2, g, w):
    depth, _, width = w.shape
    n = mem2.shape[0]
    layer = lambda *tail: pl.BlockSpec((None,) + tail, lambda l: (l, 0, 0))
    return pl.pallas_call(
        _mem_kv_kernel,
        grid=(depth,),
        in_specs=[pl.BlockSpec((n, D_MODEL), lambda l: (0, 0)), layer(1, D_MODEL), layer(D_MODEL, width)],
        out_specs=layer(n, width),
        out_shape=jax.ShapeDtypeStruct((depth, n, width), BF16),
        compiler_params=_params("parallel"),
        name="mem_kv",
    )(mem2, g, w)


def _post_kernel(x_ref, na_ref, ga_ref, wo32_ref, g_ref, wq32_ref, kv_ref, wmo32_ref, wup32_ref, wdn32_ref,
                 o_ref, wup_ref, wdn_ref, wo_ref, wq_ref, wmo_ref):
    _cast_once([(wo32_ref, wo_ref), (wq32_ref, wq_ref), (wmo32_ref, wmo_ref)])
    wup_ref[...] = wup32_ref[...].astype(BF16)
    wdn_ref[...] = wdn32_ref[...].astype(BF16)
    x1 = x_ref[...] + _dot(na_ref[...], wo_ref[:NA_WIDTH, :]) + _dot(ga_ref[...], wo_ref[NA_WIDTH:, :])
    h = _rms(x1, g_ref[...]).astype(BF16)
    q = (_dot(h, wq_ref[...]) * (MEM_HEAD_DIM ** -0.5)).astype(BF16)
    heads = []
    for hh in range(MEM_HEADS):
        cols = slice(hh * MEM_HEAD_DIM, (hh + 1) * MEM_HEAD_DIM)
        s = _dot_nt(q[:, cols], kv_ref[:, cols])
        m = jnp.max(s, axis=-1, keepdims=True)
        p = jnp.exp(s - m)
        l = jnp.sum(p, axis=-1, keepdims=True)
        vcols = slice(MEM_WIDTH + hh * MEM_HEAD_DIM, MEM_WIDTH + (hh + 1) * MEM_HEAD_DIM)
        heads.append((_dot(p.astype(BF16), kv_ref[:, vcols]) / l).astype(BF16))
    o_ref[...] = x1 + _dot(jnp.concatenate(heads, axis=-1), wmo_ref[...])


def _post(x2, na, ga, wo, g, wq, kv, wmo, wup, wdn, l, seq, mem_len):
    n = x2.shape[0]
    steps = n // TM_PROJ
    tiles_per_seq = seq // TM_PROJ
    row = lambda width: pl.BlockSpec((TM_PROJ, width), lambda i: (i, 0))
    up_rows, dn_rows = wup.shape[1] // steps, wdn.shape[1] // steps
    assert up_rows * steps == wup.shape[1] and dn_rows * steps == wdn.shape[1]
    return pl.pallas_call(
        _post_kernel,
        grid=(steps,),
        in_specs=[row(D_MODEL), row(NA_WIDTH), row(GQA_WIDTH), _layer_spec(wo, l),
                  _layer_spec(g, l), _layer_spec(wq, l),
                  pl.BlockSpec((None, mem_len, 2 * MEM_WIDTH), lambda i: (l, i // tiles_per_seq, 0)),
                  _layer_spec(wmo, l),
                  pl.BlockSpec((None, up_rows, wup.shape[2]), lambda i: (l, i, 0)),
                  pl.BlockSpec((None, dn_rows, wdn.shape[2]), lambda i: (l, i, 0))],
        out_specs=[row(D_MODEL),
                   pl.BlockSpec((up_rows, wup.shape[2]), lambda i: (i, 0)),
                   pl.BlockSpec((dn_rows, wdn.shape[2]), lambda i: (i, 0))],
        out_shape=[jax.ShapeDtypeStruct((n, D_MODEL), F32),
                   jax.ShapeDtypeStruct(wup.shape[1:], BF16),
                   jax.ShapeDtypeStruct(wdn.shape[1:], BF16)],
        scratch_shapes=[pltpu.VMEM(w.shape[1:], BF16) for w in (wo, wq, wmo)],
        compiler_params=_params("arbitrary"),
        name="post_mixer_memattn",
    )(x2, na, ga, wo, g, wq, kv, wmo, wup, wdn)


def _ffn_kernel(xp_ref, x_ref, xn_ref, g_ref, wup_ref, cw_ref, cb_ref, wdn_ref, gf_ref, o_ref,
                *, tiles_per_seq, final_norm):
    i = pl.program_id(0)
    first = (i % tiles_per_seq) == 0
    last = (i % tiles_per_seq) == tiles_per_seq - 1
    x = x_ref[...]
    tm = x.shape[0]
    xa = jnp.concatenate([xp_ref[...], x, xn_ref[...]], axis=0)
    ha = _rms(xa, g_ref[...])
    n_all = tm + 2 * HALO
    rid = lax.broadcasted_iota(jnp.int32, (n_all, 1), 0)
    pad = (first & (rid < HALO)) | (last & (rid >= tm + HALO))
    ha = jnp.where(pad, 0.0, ha).astype(BF16)

    def conv(u, c0, width):
        cols = slice(c0, c0 + width)
        prev = pltpu.roll(u, 1, 0)[HALO:HALO + tm]
        nxt = pltpu.roll(u, n_all - 1, 0)[HALO:HALO + tm]
        return (prev * cw_ref[0:1, cols] + u[HALO:HALO + tm] * cw_ref[1:2, cols]
                + nxt * cw_ref[2:3, cols] + cb_ref[:, cols])

    def up(c):
        c0, width = FF_CHUNKS[c]
        return (_dot(ha, wup_ref[:, c0:c0 + width]), _dot(ha, wup_ref[:, D_FF + c0:D_FF + c0 + width]))

    acc = x
    u_next = up(0)
    for c, (c0, width) in enumerate(FF_CHUNKS):
        ug, uv = u_next
        if c + 1 < len(FF_CHUNKS):
            u_next = up(c + 1)
        gate = conv(ug, c0, width)
        val = conv(uv, D_FF + c0, width)
        act = (gate * jax.nn.sigmoid(gate) * val).astype(BF16)
        acc = acc + _dot(act, wdn_ref[c0:c0 + width, :])
    if final_norm:
        acc = _rms(acc, gf_ref[...])
    o_ref[...] = acc


def _ffn(x2, g, wup, cw, cb, wdn, gf, l, seq, final_norm):
    n = x2.shape[0]
    tiles_per_seq = seq // TM_PROJ
    hb = TM_PROJ // HALO
    nh = n // HALO
    return pl.pallas_call(
        functools.partial(_ffn_kernel, tiles_per_seq=tiles_per_seq, final_norm=final_norm),
        grid=(n // TM_PROJ,),
        in_specs=[pl.BlockSpec((HALO, D_MODEL), lambda i: (jnp.maximum(i * hb - 1, 0), 0)),
                  pl.BlockSpec((TM_PROJ, D_MODEL), lambda i: (i, 0)),
                  pl.BlockSpec((HALO, D_MODEL), lambda i: (jnp.minimum((i + 1) * hb, nh - 1), 0)),
                  _layer_spec(g, l), _const_spec(wup.shape), _layer_spec(cw, l),
                  _layer_spec(cb, l), _const_spec(wdn.shape), _const_spec((1, D_MODEL))],
        out_specs=pl.BlockSpec((TM_PROJ, D_MODEL), lambda i: (i, 0)),
        out_shape=jax.ShapeDtypeStruct((n, D_MODEL), F32),
        compiler_params=_params("parallel"),
        name="conv_ffn",
    )(x2, x2, x2, g, wup, cw, cb, wdn, gf)


def kernel(x, mem, norm_mix, w_in, na_rpb, gqa_q_norm, gqa_k_norm, w_out, norm_mem_q, norm_mem_kv,
           w_mem_q, w_mem_kv, w_mem_o, norm_ffn, w_up, conv_w, conv_b, w_down, norm_final):
    batch, seq, d = x.shape
    depth = w_in.shape[0]
    mem_len = mem.shape[1]
    rows = seq // GRID_W
    assert d == D_MODEL and seq % GRID_W == 0 and rows >= NA_KH and rows % NA_ROWS_PER_STEP == 0
    assert seq % TM_PROJ == 0 and seq % GQA_TQ == 0

    cos, sin = _rope_tables(seq)
    x2 = x.reshape(batch * seq, d)
    mem2 = mem.reshape(batch * mem_len, d)
    rows_of = lambda a: a.reshape(a.shape[0], 1, -1)
    lanes2 = lambda a: jnp.tile(a, (1, LANES // HEAD_DIM)).reshape(a.shape[0], 1, LANES)
    norm_mix, norm_mem_q, norm_mem_kv, norm_ffn, conv_b = map(
        rows_of, (norm_mix, norm_mem_q, norm_mem_kv, norm_ffn, conv_b))
    q_gain, k_gain = lanes2(gqa_q_norm), lanes2(gqa_k_norm)
    rpb = na_rpb.reshape(depth, -1)
    kv = _mem_kv(mem2, norm_mem_kv, w_mem_kv)

    for l in range(depth):
        naq, nak, nav, gq, gk, gv = _in_proj(x2, norm_mix, w_in, l)
        na = _na_attention(rpb[l], naq, nak, nav, batch, seq)
        ga = _gqa_attention(gq, gk, gv, cos, sin, q_gain, k_gain, l, batch, seq)
        x2, w_up_l, w_down_l = _post(x2, na, ga, w_out, norm_mem_q, w_mem_q, kv, w_mem_o, w_up, w_down,
                                     l, seq, mem_len)
        x2 = _ffn(x2, norm_ffn, w_up_l, conv_w, conv_b, w_down_l, norm_final.reshape(1, -1), l, seq,
                  final_norm=(l == depth - 1))
    return x2.reshape(batch, seq, d)
```
